```python
import math
import jax, jax.numpy as jnp
from jax import lax
import numpy as np


D_MODEL = 1024
BATCH = 4
SEQ = 8192
DEPTH = 1

SSM_WIDTH = D_MODEL // 2
SSM_GROUP = 16
SSM_GROUPS = SSM_WIDTH // SSM_GROUP
SSM_STATE = 64
SB_HEAD_DIM = 64
SB_WIDTH = D_MODEL // 2
SB_HEADS = SB_WIDTH // SB_HEAD_DIM
MIX_WIDTH = SSM_WIDTH + SB_WIDTH
IN_WIDTH = SSM_WIDTH + 3 * SB_WIDTH
Q_BLOCK = 128
MEM_LEN = 256
XA_HEADS = 4
XA_HEAD_DIM = 128
XA_WIDTH = XA_HEADS * XA_HEAD_DIM
D_FF = 4 * D_MODEL
NORM_EPS = 1e-6
DT_MIN = 1e-3
DT_MAX = 1e-1

kernel_name = "hymba_s5_stickbreaking_block"


def rmsnorm(x, g):
    xf = x.astype(jnp.float32)
    xf = xf * lax.rsqrt(jnp.mean(xf * xf, axis=-1, keepdims=True) + NORM_EPS)
    return (xf * g.astype(jnp.float32)).astype(x.dtype)


def _linear_recurrence_combine(left, right):
    a_l, b_l = left
    a_r, b_r = right
    return a_r * a_l, a_r * b_l + b_r


def s5_mixer(u, a_re, a_im, log_dt, b_re, b_im, c_re, c_im, d_skip, w_glu):
    bsz, seq, _ = u.shape
    f32 = jnp.float32
    uf = u.astype(f32).reshape(bsz, seq, SSM_GROUPS, SSM_GROUP)
    lam = lax.complex(a_re.astype(f32), a_im.astype(f32))
    dt = jnp.exp(log_dt.astype(f32))[:, None]
    a_bar = jnp.exp(lam * dt)
    b = lax.complex(b_re.astype(f32), b_im.astype(f32))
    b_bar = ((a_bar - 1.0) / lam)[..., None] * b
    c = lax.complex(c_re.astype(f32), c_im.astype(f32))
    bu = jnp.einsum('bsgc,gpc->bsgp', uf.astype(jnp.complex64), b_bar)
    a_seq = jnp.broadcast_to(a_bar[None, None], (1, seq, SSM_GROUPS, SSM_STATE))
    _, states = lax.associative_scan(_linear_recurrence_combine, (a_seq, bu), axis=1)
    y = jnp.einsum('bsgp,gcp->bsgc', states, c).real + d_skip.astype(f32).reshape(SSM_GROUPS, SSM_GROUP) * uf
    y = jax.nn.gelu(y.reshape(bsz, seq, SSM_WIDTH))
    y = y * jax.nn.sigmoid(y @ w_glu.astype(f32))
    return y.astype(u.dtype)


def stick_breaking_attention(q, k, v):
    bsz, seq, h, dh = q.shape
    nblk = seq // Q_BLOCK
    scale = dh ** -0.5
    qb = q.reshape(bsz, nblk, Q_BLOCK, h, dh).transpose(1, 0, 2, 3, 4)
    key_pos = jnp.arange(seq)

    def block(args):
        qi, i = args
        logits = jnp.einsum('bqhd,bkhd->bhqk', qi, k, preferred_element_type=jnp.float32) * scale
        q_pos = i * Q_BLOCK + jnp.arange(Q_BLOCK)
        mask = key_pos[None, :] < q_pos[:, None]
        log_beta = jax.nn.log_sigmoid(logits)
        log_1m_beta = jnp.where(mask, jax.nn.log_sigmoid(-logits), 0.0)
        after = lax.cumsum(log_1m_beta, axis=3, reverse=True) - log_1m_beta
        w = jnp.where(mask, jnp.exp(log_beta + after), 0.0)
        return jnp.einsum('bhqk,bkhd->bqhd', w.astype(v.dtype), v)

    out = lax.map(block, (qb, jnp.arange(nblk)))
    return out.transpose(1, 0, 2, 3, 4).reshape(bsz, seq, h, dh)


def memory_cross_attention(h, mem_n, w_q, w_kv, g_q, g_k, w_o):
    bsz, seq, _ = h.shape
    mlen = mem_n.shape[1]
    q = (h @ w_q).reshape(bsz, seq, XA_HEADS, XA_HEAD_DIM)
    kv = mem_n @ w_kv
    k = kv[..., :XA_WIDTH].reshape(bsz, mlen, XA_HEADS, XA_HEAD_DIM)
    v = kv[..., XA_WIDTH:].reshape(bsz, mlen, XA_HEADS, XA_HEAD_DIM)
    q = rmsnorm(q, g_q)
    k = rmsnorm(k, g_k)
    scores = jnp.einsum('bqhd,bkhd->bhqk', q, k, preferred_element_type=jnp.float32) * (XA_HEAD_DIM ** -0.5)
    p = jax.nn.softmax(scores, axis=-1).astype(v.dtype)
    o = jnp.einsum('bhqk,bkhd->bqhd', p, v).reshape(bsz, seq, XA_WIDTH)
    return o @ w_o


def setup_inputs(seed: int = 0) -> dict:
    key = jax.random.key(seed)
    ks = jax.random.split(key, 32)

    def nrm(k, shape, scale):
        return jax.random.normal(k, shape, jnp.float32) * scale

    def gain(k, shape):
        return 1.0 + 0.01 * jax.random.normal(k, shape, jnp.float32)

    L = DEPTH
    n = jnp.arange(SSM_STATE, dtype=jnp.float32)
    return {
        "x": nrm(ks[0], (BATCH, SEQ, D_MODEL), 1.0),
        "mem": nrm(ks[1], (BATCH, MEM_LEN, D_MODEL), 1.0),
        "g_mix": gain(ks[2], (L, D_MODEL)),
        "w_in": nrm(ks[3], (L, D_MODEL, IN_WIDTH), D_MODEL ** -0.5),
        "ssm_a_re": -0.5 + 0.01 * jax.random.normal(ks[4], (L, SSM_GROUPS, SSM_STATE), jnp.float32),
        "ssm_a_im": jnp.pi * n + 0.01 * jax.random.normal(ks[5], (L, SSM_GROUPS, SSM_STATE), jnp.float32),
        "ssm_log_dt": jax.random.uniform(ks[6], (L, SSM_GROUPS), jnp.float32, math.log(DT_MIN), math.log(DT_MAX)),
        "ssm_b_re": nrm(ks[7], (L, SSM_GROUPS, SSM_STATE, SSM_GROUP), (2 * SSM_GROUP) ** -0.5),
        "ssm_b_im": nrm(ks[8], (L, SSM_GROUPS, SSM_STATE, SSM_GROUP), (2 * SSM_GROUP) ** -0.5),
        "ssm_c_re": nrm(ks[9], (L, SSM_GROUPS, SSM_GROUP, SSM_STATE), 0.5),
        "ssm_c_im": nrm(ks[10], (L, SSM_GROUPS, SSM_GROUP, SSM_STATE), 0.5),
        "ssm_d": nrm(ks[11], (L, SSM_WIDTH), 1.0),
        "ssm_w_glu": nrm(ks[12], (L, SSM_WIDTH, SSM_WIDTH), SSM_WIDTH ** -0.5),
        "sb_g_q": gain(ks[13], (L, SB_HEAD_DIM)),
        "sb_g_k": gain(ks[14], (L, SB_HEAD_DIM)),
        "g_out_ssm": gain(ks[15], (L, SSM_WIDTH)),
        "g_out_sb": gain(ks[16], (L, SB_WIDTH)),
        "w_out": nrm(ks[17], (L, MIX_WIDTH, D_MODEL), MIX_WIDTH ** -0.5),
        "g_xa": gain(ks[18], (L, D_MODEL)),
        "g_mem": gain(ks[19], (L, D_MODEL)),
        "xa_w_q": nrm(ks[20], (L, D_MODEL, XA_WIDTH), D_MODEL ** -0.5),
        "xa_w_kv": nrm(ks[21], (L, D_MODEL, 2 * XA_WIDTH), D_MODEL ** -0.5),
        "xa_g_q": gain(ks[22], (L, XA_HEAD_DIM)),
        "xa_g_k": gain(ks[23], (L, XA_HEAD_DIM)),
        "xa_w_o": nrm(ks[24], (L, XA_WIDTH, D_MODEL), XA_WIDTH ** -0.5),
        "g_mlp": gain(ks[25], (L, D_MODEL)),
        "w_up": nrm(ks[26], (L, D_MODEL, D_FF), D_MODEL ** -0.5),
        "w_down": nrm(ks[27], (L, D_FF, D_MODEL), D_FF ** -0.5),
    }


def reference(x, mem, g_mix, w_in, ssm_a_re, ssm_a_im, ssm_log_dt, ssm_b_re, ssm_b_im,
              ssm_c_re, ssm_c_im, ssm_d, ssm_w_glu, sb_g_q, sb_g_k, g_out_ssm, g_out_sb,
              w_out, g_xa, g_mem, xa_w_q, xa_w_kv, xa_g_q, xa_g_k, xa_w_o, g_mlp, w_up, w_down):
    bsz, seq, _ = x.shape
    for l in range(DEPTH):
        h = rmsnorm(x, g_mix[l])
        proj = h @ w_in[l]
        u = proj[..., :SSM_WIDTH]
        q = proj[..., SSM_WIDTH:SSM_WIDTH + SB_WIDTH].reshape(bsz, seq, SB_HEADS, SB_HEAD_DIM)
        k = proj[..., SSM_WIDTH + SB_WIDTH:SSM_WIDTH + 2 * SB_WIDTH].reshape(bsz, seq, SB_HEADS, SB_HEAD_DIM)
        v = proj[..., SSM_WIDTH + 2 * SB_WIDTH:].reshape(bsz, seq, SB_HEADS, SB_HEAD_DIM)
        y_ssm = s5_mixer(u, ssm_a_re[l], ssm_a_im[l], ssm_log_dt[l], ssm_b_re[l], ssm_b_im[l],
                         ssm_c_re[l], ssm_c_im[l], ssm_d[l], ssm_w_glu[l])
        q = rmsnorm(q, sb_g_q[l])
        k = rmsnorm(k, sb_g_k[l])
        y_sb = stick_breaking_attention(q, k, v).reshape(bsz, seq, SB_WIDTH)
        y = jnp.concatenate([rmsnorm(y_ssm, g_out_ssm[l]), rmsnorm(y_sb, g_out_sb[l])], axis=-1)
        x = x + y @ w_out[l]
        h = rmsnorm(x, g_xa[l])
        mem_n = rmsnorm(mem, g_mem[l])
        x = x + memory_cross_attention(h, mem_n, xa_w_q[l], xa_w_kv[l], xa_g_q[l], xa_g_k[l], xa_w_o[l])
        h = rmsnorm(x, g_mlp[l])
        x = x + jnp.square(jax.nn.relu(h @ w_up[l])) @ w_down[l]
    return x
```

```python
import functools

import jax
import jax.numpy as jnp
from jax import lax
from jax.experimental import pallas as pl
from jax.experimental.pallas import tpu as pltpu

F32 = jnp.float32
BF16 = jnp.bfloat16

NORM_EPS = 1e-6
SSM_GROUP = 16
SSM_STATE = 64
SSM_CHUNK = 16
SB_HEAD_DIM = 64
XA_HEADS = 4
XA_HEAD_DIM = 128
LANES = 128
VMEM_LIMIT_BYTES = 56 * 1024 * 1024


def _params(*sem):
    return pltpu.CompilerParams(dimension_semantics=sem, vmem_limit_bytes=VMEM_LIMIT_BYTES)


def _const_spec(shape):
    zeros = (0,) * len(shape)
    return pl.BlockSpec(shape, lambda *_: zeros, pipeline_mode=pl.Buffered(1))


def _rms(x, g):
    return x * lax.rsqrt(jnp.mean(x * x, axis=-1, keepdims=True) + NORM_EPS) * g


def _token_tile(n, want):
    t = min(n, want)
    assert n % t == 0
    return t


def _in_proj_kernel(x_ref, g_ref, w_ref, seg_ref, gq_ref, gk_ref,
                    u_ref, q_ref, k_ref, v_ref, *, ssm_w, sb_w):
    h = _rms(x_ref[...], g_ref[...])
    proj = jnp.dot(h.astype(BF16), w_ref[...], preferred_element_type=F32)
    u_ref[...] = proj[:, :ssm_w]
    q = proj[:, ssm_w:ssm_w + sb_w]
    k = proj[:, ssm_w + sb_w:ssm_w + 2 * sb_w]
    v_ref[...] = proj[:, ssm_w + 2 * sb_w:].astype(BF16)
    seg = seg_ref[...]

    def head_norm(t, g):
        ms = jnp.dot((t * t).astype(BF16), seg, preferred_element_type=F32)
        return t * lax.rsqrt(ms + NORM_EPS) * g

    q_ref[...] = head_norm(q, gq_ref[...]).astype(BF16)
    k_ref[...] = head_norm(k, gk_ref[...]).astype(BF16)


def _in_proj(x2, g_mix, w_in, g_q, g_k, ssm_w, sb_w):
    n, d = x2.shape
    tm = _token_tile(n, 512)
    heads = sb_w // SB_HEAD_DIM
    lane_head = jnp.arange(sb_w) // SB_HEAD_DIM
    seg = (lane_head[:, None] == lane_head[None, :]).astype(F32) / SB_HEAD_DIM
    gq = (jnp.tile(g_q.astype(F32), heads) * (SB_HEAD_DIM ** -0.5))[None]
    gk = jnp.tile(g_k.astype(F32), heads)[None]
    kern = functools.partial(_in_proj_kernel, ssm_w=ssm_w, sb_w=sb_w)
    tok = lambda w: pl.BlockSpec((tm, w), lambda i: (i, 0))
    return pl.pallas_call(
        kern,
        grid=(n // tm,),
        in_specs=[tok(d), _const_spec((1, d)), _const_spec(w_in.shape), _const_spec((sb_w, sb_w)),
                  _const_spec((1, sb_w)), _const_spec((1, sb_w))],
        out_specs=[tok(ssm_w), tok(sb_w), tok(sb_w), tok(sb_w)],
        out_shape=[jax.ShapeDtypeStruct((n, ssm_w), F32)] + [jax.ShapeDtypeStruct((n, sb_w), BF16)] * 3,
        compiler_params=_params("parallel"),
        name="in_proj",
    )(x2, g_mix.astype(F32)[None], w_in.astype(BF16), seg.astype(BF16), gq, gk)


def _s5_mats(a_re, a_im, log_dt, b_re, b_im, c_re, c_im):
    hp = lax.Precision.HIGHEST
    L = SSM_CHUNK
    g, p = a_re.shape
    nc = b_re.shape[-1]
    lam = lax.complex(a_re.astype(F32), a_im.astype(F32))
    ldt = lam * jnp.exp(log_dt.astype(F32))[:, None]
    a_bar = jnp.exp(ldt)
    b_bar = ((a_bar - 1.0) / lam)[..., None] * lax.complex(b_re.astype(F32), b_im.astype(F32))
    c = lax.complex(c_re.astype(F32), c_im.astype(F32))
    n = jnp.arange(L + 1, dtype=F32)
    pw = jnp.exp(ldt[None] * n[:, None, None])
    kern = jnp.einsum('gcp,ngp,gpd->ngcd', c, pw[:L], b_bar, precision=hp).real
    j = jnp.arange(L)[:, None]
    i = jnp.arange(L)[None, :]
    toe = jnp.where((i >= j)[:, :, None, None, None], kern[jnp.clip(i - j, 0, L - 1)], 0.0)
    t_mat = toe.transpose(2, 0, 4, 1, 3).reshape(g, L * nc, L * nc)
    pc = jnp.einsum('jgp,gpd->gjdp', pw[L - 1 - jnp.arange(L)], b_bar, precision=hp)
    pc = pc.reshape(g, L * nc, p)
    p_mat = jnp.concatenate([pc.real, pc.imag, pc.imag, pc.real], axis=-1)
    qc = jnp.einsum('gcp,igp->gpic', c, pw[1:L + 1], precision=hp).reshape(g, p, L * nc)
    q_mat = jnp.concatenate([qc.real, -qc.imag], axis=1)
    al = pw[L]
    coef = jnp.stack([jnp.concatenate([al.real, al.real], -1),
                      jnp.concatenate([-al.imag, al.imag], -1)], axis=1)
    return t_mat.astype(BF16), p_mat.astype(BF16), q_mat.astype(BF16), coef.astype(F32)


def _s5_kernel(u_ref, t_ref, p_ref, q_ref, coef_ref, y_ref, e_scr, s_scr, *, nb, nk, sw):
    u = u_ref[0]
    e_scr[...] = jnp.dot(u, p_ref[0], preferred_element_type=F32)
    c1 = coef_ref[0, 0:1, :]
    c2 = coef_ref[0, 1:2, :]

    def step(k, carry):
        out = []
        for b in range(nb):
            s, w = carry[b]
            row = b * nk + k
            s_scr[pl.ds(row, 1), :] = s
            e = e_scr[pl.ds(row, 1), :]
            out.append((s * c1 + w * c2 + e[:, :sw], w * c1 - s * c2 + e[:, sw:]))
        return tuple(out)

    zero = jnp.zeros((1, sw), F32)
    lax.fori_loop(0, nk, step, tuple((zero, zero) for _ in range(nb)), unroll=4)
    y = jnp.dot(u, t_ref[0], preferred_element_type=F32)
    y += jnp.dot(s_scr[...].astype(BF16), q_ref[0], preferred_element_type=F32)
    y_ref[0] = y


def _s5(u, mats):
    t_mat, p_mat, q_mat, coef = mats
    bsz, seq, width = u.shape
    g = width // SSM_GROUP
    L = SSM_CHUNK
    nk = seq // L
    rows, lw, sw = bsz * nk, L * SSM_GROUP, 2 * SSM_STATE
    ug = u.astype(BF16).reshape(bsz, nk, L, g, SSM_GROUP).transpose(3, 0, 1, 2, 4).reshape(g, rows, lw)
    kern = functools.partial(_s5_kernel, nb=bsz, nk=nk, sw=sw)
    per_g = lambda a, b: pl.BlockSpec((1, a, b), lambda i: (i, 0, 0))
    y = pl.pallas_call(
        kern,
        grid=(g,),
        in_specs=[per_g(rows, lw), per_g(lw, lw), per_g(lw, 2 * sw), per_g(sw, lw), per_g(2, sw)],
        out_specs=per_g(rows, lw),
        out_shape=jax.ShapeDtypeStruct((g, rows, lw), F32),
        scratch_shapes=[pltpu.VMEM((rows, 2 * sw), F32), pltpu.VMEM((rows, sw), F32)],
        compiler_params=_params("parallel"),
        name="s5_scan",
    )(ug, t_mat, p_mat, q_mat, coef)
    return y.reshape(g, bsz, nk, L, SSM_GROUP).transpose(1, 2, 3, 0, 4).reshape(bsz, seq, width)


def _sb_attn_kernel(q_ref, k_ref, v_ref, tri_ref, o_ref, *, blk):
    i = pl.program_id(2)
    q = q_ref[0]
    lane = lax.broadcasted_iota(jnp.int32, q.shape, 1)
    first = lane < SB_HEAD_DIM
    zq = jnp.zeros_like(q)
    qh = (jnp.where(first, q, zq), jnp.where(first, zq, q))
    tri = tri_ref[...]
    row = lax.broadcasted_iota(jnp.int32, (blk, blk), 0)
    col = lax.broadcasted_iota(jnp.int32, (blk, blk), 1)
    below = col < row

    def tile(qv, kb, vb, carry, acc, mask):
        z = lax.dot_general(qv, kb, (((1,), (1,)), ((), ())), preferred_element_type=F32)
        sp = jnp.maximum(z, 0.0) + jnp.log(1.0 + jnp.exp(-jnp.abs(z)))
        if mask is not None:
            sp = jnp.where(mask, sp, 0.0)
        hi = sp.astype(BF16)
        lo = (sp - hi.astype(F32)).astype(BF16)
        cum = (jnp.dot(hi, tri, preferred_element_type=F32)
               + jnp.dot(lo, tri, preferred_element_type=F32))
        w = jnp.exp(z - cum - carry)
        if mask is not None:
            w = jnp.where(mask, w, 0.0)
        acc = acc + jnp.dot(w.astype(BF16), vb, preferred_element_type=F32)
        return carry + cum[:, 0:1], acc

    def block(j, state, mask):
        start = pl.multiple_of(j * blk, blk)
        kb = k_ref[0, pl.ds(start, blk), :]
        vb = v_ref[0, pl.ds(start, blk), :]
        c0, a0, c1, a1 = state
        c0, a0 = tile(qh[0], kb, vb, c0, a0, mask)
        c1, a1 = tile(qh[1], kb, vb, c1, a1, mask)
        return c0, a0, c1, a1

    zc = jnp.zeros((blk, 1), F32)
    za = jnp.zeros((blk, LANES), F32)
    state = block(i, (zc, za, zc, za), below)
    state = lax.fori_loop(0, i, lambda n, st: block(i - 1 - n, st, None), state)
    o_ref[0] = jnp.where(first, state[1], state[3]).astype(o_ref.dtype)


def _sb_attn(q, k, v):
    bsz, seq, width = q.shape
    blk = _token_tile(seq, 256)
    assert width % LANES == 0
    tri = (jnp.arange(blk)[:, None] >= jnp.arange(blk)[None, :]).astype(BF16)
    kern = functools.partial(_sb_attn_kernel, blk=blk)
    return pl.pallas_call(
        kern,
        grid=(bsz, width // LANES, seq // blk),
        in_specs=[pl.BlockSpec((1, blk, LANES), lambda b, h, i: (b, i, h)),
                  pl.BlockSpec((1, seq, LANES), lambda b, h, i: (b, 0, h)),
                  pl.BlockSpec((1, seq, LANES), lambda b, h, i: (b, 0, h)),
                  _const_spec((blk, blk))],
        out_specs=pl.BlockSpec((1, blk, LANES), lambda b, h, i: (b, i, h)),
        out_shape=jax.ShapeDtypeStruct((bsz, seq, width), BF16),
        compiler_params=_params("parallel", "parallel", "arbitrary"),
        name="sb_attn",
    )(q, k, v, tri)


def _mix_out_kernel(x_ref, ys_ref, u_ref, sb_ref, d_ref, wg_ref, gs_ref, gb_ref, wo_ref, o_ref, *, ssm_w):
    y = ys_ref[...] + d_ref[...] * u_ref[...]
    y = jax.nn.gelu(y)
    gate = jnp.dot(y.astype(BF16), wg_ref[...], preferred_element_type=F32)
    y = y * jax.nn.sigmoid(gate)
    yn = _rms(y, gs_ref[...]).astype(BF16)
    sn = _rms(sb_ref[...].astype(F32), gb_ref[...]).astype(BF16)
    out = jnp.dot(yn, wo_ref[:ssm_w, :], preferred_element_type=F32)
    out += jnp.dot(sn, wo_ref[ssm_w:, :], preferred_element_type=F32)
    o_ref[...] = x_ref[...] + out


def _mix_out(x2, y_s5, u, y_sb, d_skip, w_glu, g_ssm, g_sb, w_out):
    n, d = x2.shape
    ssm_w, sb_w = y_s5.shape[1], y_sb.shape[1]
    tm = _token_tile(n, 512)
    kern = functools.partial(_mix_out_kernel, ssm_w=ssm_w)
    tok = lambda w: pl.BlockSpec((tm, w), lambda i: (i, 0))
    return pl.pallas_call(
        kern,
        grid=(n // tm,),
        in_specs=[tok(d), tok(ssm_w), tok(ssm_w), tok(sb_w), _const_spec((1, ssm_w)),
                  _const_spec(w_glu.shape), _const_spec((1, ssm_w)), _const_spec((1, sb_w)),
                  _const_spec(w_out.shape)],
        out_specs=tok(d),
        out_shape=jax.ShapeDtypeStruct((n, d), F32),
        compiler_params=_params("parallel"),
        name="mix_out",
    )(x2, y_s5, u, y_sb, d_skip.astype(F32)[None], w_glu.astype(BF16), g_ssm.astype(F32)[None],
      g_sb.astype(F32)[None], w_out.astype(BF16))


def _mem_kv_kernel(m_ref, g_ref, w_ref, gk_ref, k_ref, v_ref, *, xa_w):
    h = _rms(m_ref[0], g_ref[...])
    kv = jnp.dot(h.astype(BF16), w_ref[...], preferred_element_type=F32)
    for hd in range(xa_w // XA_HEAD_DIM):
        sl = slice(hd * XA_HEAD_DIM, (hd + 1) * XA_HEAD_DIM)
        k_ref[0, :, sl] = _rms(kv[:, sl], gk_ref[...]).astype(BF16)
    v_ref[0] = kv[:, xa_w:].astype(BF16)


def _mem_kv(mem, g_mem, w_kv, g_k):
    bsz, mlen, d = mem.shape
    xa_w = w_kv.shape[1] // 2
    kern = functools.partial(_mem_kv_kernel, xa_w=xa_w)
    per_b = lambda w: pl.BlockSpec((1, mlen, w), lambda b: (b, 0, 0))
    return pl.pallas_call(
        kern,
        grid=(bsz,),
        in_specs=[per_b(d), _const_spec((1, d)), _const_spec(w_kv.shape), _const_spec((1, XA_HEAD_DIM))],
        out_specs=[per_b(xa_w), per_b(xa_w)],
        out_shape=[jax.ShapeDtypeStruct((bsz, mlen, xa_w), BF16)] * 2,
        compiler_params=_params("parallel"),
        name="mem_kv",
    )(mem, g_mem.astype(F32)[None], w_kv.astype(BF16), g_k.astype(F32)[None])


def _xattn_kernel(x_ref, g_ref, wq_ref, gq_ref, k_ref, v_ref, wo_ref, o_ref):
    x = x_ref[0]
    h = _rms(x, g_ref[...])
    q = jnp.dot(h.astype(BF16), wq_ref[...], preferred_element_type=F32)
    heads = []
    for hd in range(q.shape[1] // XA_HEAD_DIM):
        sl = slice(hd * XA_HEAD_DIM, (hd + 1) * XA_HEAD_DIM)
        qh = _rms(q[:, sl], gq_ref[...]).astype(BF16)
        s = lax.dot_general(qh, k_ref[0, :, sl], (((1,), (1,)), ((), ())), preferred_element_type=F32)
        e = jnp.exp(s - jnp.max(s, axis=-1, keepdims=True))
        pv = jnp.dot(e.astype(BF16), v_ref[0, :, sl], preferred_element_type=F32)
        heads.append((pv / jnp.sum(e, axis=-1, keepdims=True)).astype(BF16))
    o = jnp.concatenate(heads, axis=-1)
    o_ref[0] = x + jnp.dot(o, wo_ref[...], preferred_element_type=F32)


def _xattn(x, g_xa, w_q, g_q, k, v, w_o):
    bsz, seq, d = x.shape
    mlen, xa_w = k.shape[1], k.shape[2]
    tm = _token_tile(seq, 512)
    gq = (g_q.astype(F32) * (XA_HEAD_DIM ** -0.5))[None]
    return pl.pallas_call(
        _xattn_kernel,
        grid=(bsz, seq // tm),
        in_specs=[pl.BlockSpec((1, tm, d), lambda b, i: (b, i, 0)), _const_spec((1, d)),
                  _const_spec(w_q.shape), _const_spec((1, XA_HEAD_DIM)),
                  pl.BlockSpec((1, mlen, xa_w), lambda b, i: (b, 0, 0)),
                  pl.BlockSpec((1, mlen, xa_w), lambda b, i: (b, 0, 0)),
                  _const_spec(w_o.shape)],
        out_specs=pl.BlockSpec((1, tm, d), lambda b, i: (b, i, 0)),
        out_shape=jax.ShapeDtypeStruct((bsz, seq, d), F32),
        compiler_params=_params("parallel", "parallel"),
        name="xattn",
    )(x, g_xa.astype(F32)[None], w_q.astype(BF16), gq, k, v, w_o.astype(BF16))


def _mlp_kernel(x_ref, g_ref, wu_ref, wd_ref, o_ref, *, ff_chunk):
    x = x_ref[...]
    h = _rms(x, g_ref[...]).astype(BF16)
    acc = x
    for c in range(wu_ref.shape[1] // ff_chunk):
        sl = slice(c * ff_chunk, (c + 1) * ff_chunk)
        a = jnp.maximum(jnp.dot(h, wu_ref[:, sl], preferred_element_type=F32), 0.0)
        acc = acc + jnp.dot((a * a).astype(BF16), wd_ref[sl, :], preferred_element_type=F32)
    o_ref[...] = acc


def _mlp(x2, g_mlp, w_up, w_down):
    n, d = x2.shape
    tm = _token_tile(n, 512)
    ff_chunk = min(w_up.shape[1], 1024)
    kern = functools.partial(_mlp_kernel, ff_chunk=ff_chunk)
    tok = pl.BlockSpec((tm, d), lambda i: (i, 0))
    return pl.pallas_call(
        kern,
        grid=(n // tm,),
        in_specs=[tok, _const_spec((1, d)), _const_spec(w_up.shape), _const_spec(w_down.shape)],
        out_specs=tok,
        out_shape=jax.ShapeDtypeStruct((n, d), F32),
        compiler_params=_params("parallel"),
        name="mlp",
    )(x2, g_mlp.astype(F32)[None], w_up.astype(BF16), w_down.astype(BF16))


def kernel(x, mem, g_mix, w_in, ssm_a_re, ssm_a_im, ssm_log_dt, ssm_b_re, ssm_b_im, ssm_c_re, ssm_c_im,
           ssm_d, ssm_w_glu, sb_g_q, sb_g_k, g_out_ssm, g_out_sb, w_out, g_xa, g_mem, xa_w_q, xa_w_kv,
           xa_g_q, xa_g_k, xa_w_o, g_mlp, w_up, w_down):
    bsz, seq, d = x.shape
    n = bsz * seq
    ssm_w = ssm_d.shape[-1]
    sb_w = g_out_sb.shape[-1]
    x = x.astype(F32)
    for l in range(g_mix.shape[0]):
        u, q, k, v = _in_proj(x.reshape(n, d), g_mix[l], w_in[l], sb_g_q[l], sb_g_k[l], ssm_w, sb_w)
        mats = _s5_mats(ssm_a_re[l], ssm_a_im[l], ssm_log_dt[l], ssm_b_re[l], ssm_b_im[l],
                        ssm_c_re[l], ssm_c_im[l])
        y_s5 = _s5(u.reshape(bsz, seq, ssm_w), mats).reshape(n, ssm_w)
        y_sb = _sb_attn(q.reshape(bsz, seq, sb_w), k.reshape(bsz, seq, sb_w), v.reshape(bsz, seq, sb_w))
        x1 = _mix_out(x.reshape(n, d), y_s5, u, y_sb.reshape(n, sb_w), ssm_d[l], ssm_w_glu[l],
                      g_out_ssm[l], g_out_sb[l], w_out[l])
        mk, mv = _mem_kv(mem.astype(F32), g_mem[l], xa_w_kv[l], xa_g_k[l])
        x2 = _xattn(x1.reshape(bsz, seq, d), g_xa[l], xa_w_q[l], xa_g_q[l], mk, mv, xa_w_o[l])
        x = _mlp(x2.reshape(n, d), g_mlp[l], w_up[l], w_down[l]).reshape(bsz, seq, d)
    return x
```

```python
import functools

import jax
import jax.numpy as jnp
from jax import lax
from jax.experimental import pallas as pl
from jax.experimental.pallas import tpu as pltpu

F32 = jnp.float32
BF16 = jnp.bfloat16

NORM_EPS = 1e-6
SSM_GROUP = 16
SSM_STATE = 64
SSM_CHUNK = 16
SB_HEAD_DIM = 64
XA_HEADS = 4
XA_HEAD_DIM = 128
LANES = 128
LOG2E = 1.4426950408889634
SB_Q_SCALE = SB_HEAD_DIM ** -0.5 * LOG2E
SB_BOUND_SLACK = 1.05
VMEM_LIMIT_BYTES = 56 * 1024 * 1024


def _params(*sem):
    return pltpu.CompilerParams(dimension_semantics=sem, vmem_limit_bytes=VMEM_LIMIT_BYTES)


def _const_spec(shape):
    zeros = (0,) * len(shape)
    return pl.BlockSpec(shape, lambda *_: zeros, pipeline_mode=pl.Buffered(1))


def _rms(x, g):
    return x * lax.rsqrt(jnp.mean(x * x, axis=-1, keepdims=True) + NORM_EPS) * g


def _token_tile(n, want):
    t = min(n, want)
    assert n % t == 0
    return t


def _in_proj_kernel(x_ref, g_ref, w_ref, seg_ref, gq_ref, gk_ref,
                    u_ref, q_ref, k_ref, v_ref, *, ssm_w, sb_w):
    h = _rms(x_ref[...], g_ref[...])
    proj = jnp.dot(h.astype(BF16), w_ref[...], preferred_element_type=F32)
    u_ref[...] = proj[:, :ssm_w]
    q = proj[:, ssm_w:ssm_w + sb_w]
    k = proj[:, ssm_w + sb_w:ssm_w + 2 * sb_w]
    v_ref[...] = proj[:, ssm_w + 2 * sb_w:].astype(BF16)
    seg = seg_ref[...]

    def head_norm(t, g):
        ms = jnp.dot((t * t).astype(BF16), seg, preferred_element_type=F32)
        return t * lax.rsqrt(ms + NORM_EPS) * g

    q_ref[...] = head_norm(q, gq_ref[...]).astype(BF16)
    k_ref[...] = head_norm(k, gk_ref[...]).astype(BF16)


def _in_proj(x2, g_mix, w_in, g_q, g_k, ssm_w, sb_w):
    n, d = x2.shape
    tm = _token_tile(n, 512)
    heads = sb_w // SB_HEAD_DIM
    lane_head = jnp.arange(sb_w) // SB_HEAD_DIM
    seg = (lane_head[:, None] == lane_head[None, :]).astype(F32) / SB_HEAD_DIM
    gq = (jnp.tile(g_q.astype(F32), heads) * SB_Q_SCALE)[None]
    gk = jnp.tile(g_k.astype(F32), heads)[None]
    kern = functools.partial(_in_proj_kernel, ssm_w=ssm_w, sb_w=sb_w)
    tok = lambda w: pl.BlockSpec((tm, w), lambda i: (i, 0))
    return pl.pallas_call(
        kern,
        grid=(n // tm,),
        in_specs=[tok(d), _const_spec((1, d)), _const_spec(w_in.shape), _const_spec((sb_w, sb_w)),
                  _const_spec((1, sb_w)), _const_spec((1, sb_w))],
        out_specs=[tok(ssm_w), tok(sb_w), tok(sb_w), tok(sb_w)],
        out_shape=[jax.ShapeDtypeStruct((n, ssm_w), F32)] + [jax.ShapeDtypeStruct((n, sb_w), BF16)] * 3,
        compiler_params=_params("parallel"),
        name="in_proj",
    )(x2, g_mix.astype(F32)[None], w_in.astype(BF16), seg.astype(BF16), gq, gk)


def _s5_mats(a_re, a_im, log_dt, b_re, b_im, c_re, c_im):
    hp = lax.Precision.HIGHEST
    L = SSM_CHUNK
    g, p = a_re.shape
    nc = b_re.shape[-1]
    lr, li = a_re.astype(F32), a_im.astype(F32)
    dt = jnp.exp(log_dt.astype(F32))[:, None]
    n = jnp.arange(L + 1, dtype=F32)[:, None, None]
    mag = jnp.exp(n * (lr * dt)[None])
    pr, pi = mag * jnp.cos(n * (li * dt)[None]), mag * jnp.sin(n * (li * dt)[None])
    xr, xi = pr[1] - 1.0, pi[1]
    den = lr * lr + li * li
    fr, fi = (xr * lr + xi * li) / den, (xi * lr - xr * li) / den
    br, bi = b_re.astype(F32), b_im.astype(F32)
    bbr = fr[..., None] * br - fi[..., None] * bi
    bbi = fr[..., None] * bi + fi[..., None] * br
    cr, ci = c_re.astype(F32), c_im.astype(F32)
    mr = pr[..., None] * bbr[None] - pi[..., None] * bbi[None]
    mi = pr[..., None] * bbi[None] + pi[..., None] * bbr[None]
    kern = (jnp.einsum('gcp,ngpd->ngcd', cr, mr[:L], precision=hp)
            - jnp.einsum('gcp,ngpd->ngcd', ci, mi[:L], precision=hp))
    j = jnp.arange(L)[:, None]
    i = jnp.arange(L)[None, :]
    toe = jnp.where((i >= j)[:, :, None, None, None], kern[jnp.clip(i - j, 0, L - 1)], 0.0)
    t_mat = toe.transpose(2, 0, 4, 1, 3).reshape(g, L * nc, L * nc)
    rev = L - 1 - jnp.arange(L)
    pcr = mr[rev].transpose(1, 0, 3, 2).reshape(g, L * nc, p)
    pci = mi[rev].transpose(1, 0, 3, 2).reshape(g, L * nc, p)
    p_mat = jnp.concatenate([pcr, pci, pci, pcr], axis=-1)
    qr = cr[None] * pr[1:, :, None, :] - ci[None] * pi[1:, :, None, :]
    qi = cr[None] * pi[1:, :, None, :] + ci[None] * pr[1:, :, None, :]
    qr = qr.transpose(1, 3, 0, 2).reshape(g, p, L * nc)
    qi = qi.transpose(1, 3, 0, 2).reshape(g, p, L * nc)
    q_mat = jnp.concatenate([qr, -qi], axis=1)
    coef = jnp.stack([jnp.concatenate([pr[L], pr[L]], -1),
                      jnp.concatenate([-pi[L], pi[L]], -1)], axis=1)
    return t_mat.astype(BF16), p_mat.astype(BF16), q_mat.astype(BF16), coef.astype(F32)


def _s5_kernel(u_ref, t_ref, p_ref, q_ref, coef_ref, y_ref, e_scr, s_scr, *, nb, nk, sw):
    u = u_ref[0]
    e_scr[...] = jnp.dot(u, p_ref[0], preferred_element_type=F32)
    c1 = coef_ref[0, 0:1, :]
    c2 = coef_ref[0, 1:2, :]

    def step(k, carry):
        out = []
        for b in range(nb):
            s, w = carry[b]
            row = b * nk + k
            s_scr[pl.ds(row, 1), :] = s
            e = e_scr[pl.ds(row, 1), :]
            out.append((s * c1 + w * c2 + e[:, :sw], w * c1 - s * c2 + e[:, sw:]))
        return tuple(out)

    zero = jnp.zeros((1, sw), F32)
    lax.fori_loop(0, nk, step, tuple((zero, zero) for _ in range(nb)), unroll=4)
    y = jnp.dot(u, t_ref[0], preferred_element_type=F32)
    y += jnp.dot(s_scr[...].astype(BF16), q_ref[0], preferred_element_type=F32)
    y_ref[0] = y


def _s5(u, mats):
    t_mat, p_mat, q_mat, coef = mats
    bsz, seq, width = u.shape
    g = width // SSM_GROUP
    L = SSM_CHUNK
    nk = seq // L
    rows, lw, sw = bsz * nk, L * SSM_GROUP, 2 * SSM_STATE
    ug = u.astype(BF16).reshape(bsz, nk, L, g, SSM_GROUP).transpose(3, 0, 1, 2, 4).reshape(g, rows, lw)
    kern = functools.partial(_s5_kernel, nb=bsz, nk=nk, sw=sw)
    per_g = lambda a, b: pl.BlockSpec((1, a, b), lambda i: (i, 0, 0))
    y = pl.pallas_call(
        kern,
        grid=(g,),
        in_specs=[per_g(rows, lw), per_g(lw, lw), per_g(lw, 2 * sw), per_g(sw, lw), per_g(2, sw)],
        out_specs=per_g(rows, lw),
        out_shape=jax.ShapeDtypeStruct((g, rows, lw), F32),
        scratch_shapes=[pltpu.VMEM((rows, 2 * sw), F32), pltpu.VMEM((rows, sw), F32)],
        compiler_params=_params("parallel"),
        name="s5_scan",
    )(ug, t_mat, p_mat, q_mat, coef)
    return y.reshape(g, bsz, nk, L, SSM_GROUP).transpose(1, 2, 3, 0, 4).reshape(bsz, seq, width)


F32_ZERO_EXP2 = 160.0


def _sb_attn_kernel(thr_ref, q_ref, k_ref, v_ref, tri_ref, o_ref, *, blk):
    i = pl.program_id(2)
    q = q_ref[0]
    lane = lax.broadcasted_iota(jnp.int32, q.shape, 1)
    first = lane < SB_HEAD_DIM
    zq = jnp.zeros_like(q)
    qh = (jnp.where(first, q, zq), jnp.where(first, zq, q))
    tri = tri_ref[...]
    row = lax.broadcasted_iota(jnp.int32, (blk, blk), 0)
    col = lax.broadcasted_iota(jnp.int32, (blk, blk), 1)
    below = col < row

    def tile(qv, kb, vb, carry, acc, mask):
        z = lax.dot_general(qv, kb, (((1,), (1,)), ((), ())), preferred_element_type=F32)
        sp = jnp.maximum(z, 0.0) + jnp.log(1.0 + jnp.exp2(-jnp.abs(z))) * LOG2E
        if mask is not None:
            sp = jnp.where(mask, sp, 0.0)
        hi = sp.astype(BF16)
        lo = (sp - hi.astype(F32)).astype(BF16)
        cum = jnp.dot(jnp.concatenate([hi, lo], axis=1), tri, preferred_element_type=F32)
        w = jnp.exp2(z - cum - carry)
        if mask is not None:
            w = jnp.where(mask, w, 0.0)
        acc = acc + jnp.dot(w.astype(BF16), vb, preferred_element_type=F32)
        return carry + cum[:, 0:1], acc

    def block(j, state, mask):
        start = pl.multiple_of(j * blk, blk)
        kb = k_ref[0, pl.ds(start, blk), :]
        vb = v_ref[0, pl.ds(start, blk), :]
        c0, a0, c1, a1 = state
        c0, a0 = tile(qh[0], kb, vb, c0, a0, mask)
        c1, a1 = tile(qh[1], kb, vb, c1, a1, mask)
        return c0, a0, c1, a1

    zc = jnp.zeros((blk, 1), F32)
    za = jnp.zeros((blk, LANES), F32)
    state = block(i, (zc, za, zc, za), below)
    thr = thr_ref[0]

    def more(st):
        return jnp.logical_and(st[0] >= 0, st[1] <= thr)

    def step(st):
        j = st[0]
        new = block(j, st[2:], None)
        return (j - 1, jnp.min(jnp.minimum(new[0], new[2]))) + new

    state = lax.while_loop(more, step, (i - 1, jnp.float32(0.0)) + state)
    o_ref[0] = jnp.where(first, state[3], state[5]).astype(o_ref.dtype)


def _sb_attn(q, k, v, z_bound):
    bsz, seq, width = q.shape
    blk = _token_tile(seq, 256)
    assert width % LANES == 0
    tri = (jnp.arange(blk)[:, None] >= jnp.arange(blk)[None, :]).astype(BF16)
    tri = jnp.concatenate([tri, tri], axis=0)
    thr = (z_bound + F32_ZERO_EXP2).astype(F32).reshape(1)
    kern = functools.partial(_sb_attn_kernel, blk=blk)
    return pl.pallas_call(
        kern,
        grid=(bsz, width // LANES, seq // blk),
        in_specs=[pl.BlockSpec(memory_space=pltpu.SMEM),
                  pl.BlockSpec((1, blk, LANES), lambda b, h, i: (b, i, h)),
                  pl.BlockSpec((1, seq, LANES), lambda b, h, i: (b, 0, h)),
                  pl.BlockSpec((1, seq, LANES), lambda b, h, i: (b, 0, h)),
                  _const_spec((2 * blk, blk))],
        out_specs=pl.BlockSpec((1, blk, LANES), lambda b, h, i: (b, i, h)),
        out_shape=jax.ShapeDtypeStruct((bsz, seq, width), BF16),
        compiler_params=_params("parallel", "parallel", "arbitrary"),
        name="sb_attn",
    )(thr, q, k, v, tri)


def _mix_out_kernel(x_ref, ys_ref, u_ref, sb_ref, d_ref, wg_ref, gs_ref, gb_ref, wo_ref, o_ref, *, ssm_w):
    y = ys_ref[...] + d_ref[...] * u_ref[...]
    y = jax.nn.gelu(y)
    gate = jnp.dot(y.astype(BF16), wg_ref[...], preferred_element_type=F32)
    y = y * jax.nn.sigmoid(gate)
    yn = _rms(y, gs_ref[...]).astype(BF16)
    sn = _rms(sb_ref[...].astype(F32), gb_ref[...]).astype(BF16)
    out = jnp.dot(yn, wo_ref[:ssm_w, :], preferred_element_type=F32)
    out += jnp.dot(sn, wo_ref[ssm_w:, :], preferred_element_type=F32)
    o_ref[...] = x_ref[...] + out


def _mix_out(x2, y_s5, u, y_sb, d_skip, w_glu, g_ssm, g_sb, w_out):
    n, d = x2.shape
    ssm_w, sb_w = y_s5.shape[1], y_sb.shape[1]
    tm = _token_tile(n, 512)
    kern = functools.partial(_mix_out_kernel, ssm_w=ssm_w)
    tok = lambda w: pl.BlockSpec((tm, w), lambda i: (i, 0))
    return pl.pallas_call(
        kern,
        grid=(n // tm,),
        in_specs=[tok(d), tok(ssm_w), tok(ssm_w), tok(sb_w), _const_spec((1, ssm_w)),
                  _const_spec(w_glu.shape), _const_spec((1, ssm_w)), _const_spec((1, sb_w)),
                  _const_spec(w_out.shape)],
        out_specs=tok(d),
        out_shape=jax.ShapeDtypeStruct((n, d), F32),
        compiler_params=_params("parallel"),
        name="mix_out",
    )(x2, y_s5, u, y_sb, d_skip.astype(F32)[None], w_glu.astype(BF16), g_ssm.astype(F32)[None],
      g_sb.astype(F32)[None], w_out.astype(BF16))


def _mem_kv_kernel(m_ref, g_ref, w_ref, gk_ref, k_ref, v_ref, *, xa_w):
    h = _rms(m_ref[0], g_ref[...])
    kv = jnp.dot(h.astype(BF16), w_ref[...], preferred_element_type=F32)
    for hd in range(xa_w // XA_HEAD_DIM):
        sl = slice(hd * XA_HEAD_DIM, (hd + 1) * XA_HEAD_DIM)
        k_ref[0, :, sl] = _rms(kv[:, sl], gk_ref[...]).astype(BF16)
    v_ref[0] = kv[:, xa_w:].astype(BF16)


def _mem_kv(mem, g_mem, w_kv, g_k):
    bsz, mlen, d = mem.shape
    xa_w = w_kv.shape[1] // 2
    kern = functools.partial(_mem_kv_kernel, xa_w=xa_w)
    per_b = lambda w: pl.BlockSpec((1, mlen, w), lambda b: (b, 0, 0))
    return pl.pallas_call(
        kern,
        grid=(bsz,),
        in_specs=[per_b(d), _const_spec((1, d)), _const_spec(w_kv.shape), _const_spec((1, XA_HEAD_DIM))],
        out_specs=[per_b(xa_w), per_b(xa_w)],
        out_shape=[jax.ShapeDtypeStruct((bsz, mlen, xa_w), BF16)] * 2,
        compiler_params=_params("parallel"),
        name="mem_kv",
    )(mem, g_mem.astype(F32)[None], w_kv.astype(BF16), g_k.astype(F32)[None])


def _xattn_kernel(x_ref, g_ref, wq_ref, gq_ref, k_ref, v_ref, wo_ref, o_ref):
    x = x_ref[0]
    h = _rms(x, g_ref[...])
    q = jnp.dot(h.astype(BF16), wq_ref[...], preferred_element_type=F32)
    heads = []
    for hd in range(q.shape[1] // XA_HEAD_DIM):
        sl = slice(hd * XA_HEAD_DIM, (hd + 1) * XA_HEAD_DIM)
        qh = _rms(q[:, sl], gq_ref[...]).astype(BF16)
        s = lax.dot_general(qh, k_ref[0, :, sl], (((1,), (1,)), ((), ())), preferred_element_type=F32)
        e = jnp.exp(s - jnp.max(s, axis=-1, keepdims=True))
        pv = jnp.dot(e.astype(BF16), v_ref[0, :, sl], preferred_element_type=F32)
        heads.append((pv / jnp.sum(e, axis=-1, keepdims=True)).astype(BF16))
    o = jnp.concatenate(heads, axis=-1)
    o_ref[0] = x + jnp.dot(o, wo_ref[...], preferred_element_type=F32)


def _xattn(x, g_xa, w_q, g_q, k, v, w_o):
    bsz, seq, d = x.shape
    mlen, xa_w = k.shape[1], k.shape[2]
    tm = _token_tile(seq, 512)
    gq = (g_q.astype(F32) * (XA_HEAD_DIM ** -0.5))[None]
    return pl.pallas_call(
        _xattn_kernel,
        grid=(bsz, seq // tm),
        in_specs=[pl.BlockSpec((1, tm, d), lambda b, i: (b, i, 0)), _const_spec((1, d)),
                  _const_spec(w_q.shape), _const_spec((1, XA_HEAD_DIM)),
                  pl.BlockSpec((1, mlen, xa_w), lambda b, i: (b, 0, 0)),
                  pl.BlockSpec((1, mlen, xa_w), lambda b, i: (b, 0, 0)),
                  _const_spec(w_o.shape)],
        out_specs=pl.BlockSpec((1, tm, d), lambda b, i: (b, i, 0)),
        out_shape=jax.ShapeDtypeStruct((bsz, seq, d), F32),
        compiler_params=_params("parallel", "parallel"),
        name="xattn",
    )(x, g_xa.astype(F32)[None], w_q.astype(BF16), gq, k, v, w_o.astype(BF16))


def _mlp_kernel(x_ref, g_ref, wu_ref, wd_ref, o_ref, *, ff_chunk):
    x = x_ref[...]
    h = _rms(x, g_ref[...]).astype(BF16)
    acc = x
    for c in range(wu_ref.shape[1] // ff_chunk):
        sl = slice(c * ff_chunk, (c + 1) * ff_chunk)
        a = jnp.maximum(jnp.dot(h, wu_ref[:, sl], preferred_element_type=F32), 0.0)
        acc = acc + jnp.dot((a * a).astype(BF16), wd_ref[sl, :], preferred_element_type=F32)
    o_ref[...] = acc


def _mlp(x2, g_mlp, w_up, w_down):
    n, d = x2.shape
    tm = _token_tile(n, 512)
    ff_chunk = min(w_up.shape[1], 1024)
    kern = functools.partial(_mlp_kernel, ff_chunk=ff_chunk)
    tok = pl.BlockSpec((tm, d), lambda i: (i, 0))
    return pl.pallas_call(
        kern,
        grid=(n // tm,),
        in_specs=[tok, _const_spec((1, d)), _const_spec(w_up.shape), _const_spec(w_down.shape)],
        out_specs=tok,
        out_shape=jax.ShapeDtypeStruct((n, d), F32),
        compiler_params=_params("parallel"),
        name="mlp",
    )(x2, g_mlp.astype(F32)[None], w_up.astype(BF16), w_down.astype(BF16))


def kernel(x, mem, g_mix, w_in, ssm_a_re, ssm_a_im, ssm_log_dt, ssm_b_re, ssm_b_im, ssm_c_re, ssm_c_im,
           ssm_d, ssm_w_glu, sb_g_q, sb_g_k, g_out_ssm, g_out_sb, w_out, g_xa, g_mem, xa_w_q, xa_w_kv,
           xa_g_q, xa_g_k, xa_w_o, g_mlp, w_up, w_down):
    bsz, seq, d = x.shape
    n = bsz * seq
    ssm_w = ssm_d.shape[-1]
    sb_w = g_out_sb.shape[-1]
    x = x.astype(F32)
    for l in range(g_mix.shape[0]):
        u, q, k, v = _in_proj(x.reshape(n, d), g_mix[l], w_in[l], sb_g_q[l], sb_g_k[l], ssm_w, sb_w)
        mats = _s5_mats(ssm_a_re[l], ssm_a_im[l], ssm_log_dt[l], ssm_b_re[l], ssm_b_im[l],
                        ssm_c_re[l], ssm_c_im[l])
        y_s5 = _s5(u.reshape(bsz, seq, ssm_w), mats).reshape(n, ssm_w)
        z_bound = (SB_HEAD_DIM * SB_Q_SCALE * SB_BOUND_SLACK
                   * jnp.max(jnp.abs(sb_g_q[l].astype(F32))) * jnp.max(jnp.abs(sb_g_k[l].astype(F32))))
        y_sb = _sb_attn(q.reshape(bsz, seq, sb_w), k.reshape(bsz, seq, sb_w), v.reshape(bsz, seq, sb_w), z_bound)
        x1 = _mix_out(x.reshape(n, d), y_s5, u, y_sb.reshape(n, sb_w), ssm_d[l], ssm_w_glu[l],
                      g_out_ssm[l], g_out_sb[l], w_out[l])
        mk, mv = _mem_kv(mem.astype(F32), g_mem[l], xa_w_kv[l], xa_g_k[l])
        x2 = _xattn(x1.reshape(bsz, seq, d), g_xa[l], xa_w_q[l], xa_g_q[l], mk, mv, xa_w_o[l])
        x = _mlp(x2.reshape(n, d), g_mlp[l], w_up[l], w_down[l]).reshape(bsz, seq, d)
    return x
```

```python
import functools

import jax
import jax.numpy as jnp
from jax import lax
from jax.experimental import pallas as pl
from jax.experimental.pallas import tpu as pltpu

F32 = jnp.float32
BF16 = jnp.bfloat16

NORM_EPS = 1e-6
SSM_GROUP = 16
SSM_STATE = 64
SSM_CHUNK = 16
SB_HEAD_DIM = 64
XA_HEADS = 4
XA_HEAD_DIM = 128
LANES = 128
LOG2E = 1.4426950408889634
SB_Q_SCALE = SB_HEAD_DIM ** -0.5 * LOG2E
SB_BOUND_SLACK = 1.05
VMEM_LIMIT_BYTES = 56 * 1024 * 1024


def _params(*sem):
    return pltpu.CompilerParams(dimension_semantics=sem, vmem_limit_bytes=VMEM_LIMIT_BYTES)


def _const_spec(shape):
    zeros = (0,) * len(shape)
    return pl.BlockSpec(shape, lambda *_: zeros, pipeline_mode=pl.Buffered(1))


def _rms(x, g):
    return x * lax.rsqrt(jnp.mean(x * x, axis=-1, keepdims=True) + NORM_EPS) * g


def _token_tile(n, want):
    t = min(n, want)
    assert n % t == 0
    return t


ROWS_PER_STEP = 32


def _in_proj_kernel(x_ref, g_ref, w_ref, seg_ref, gq_ref, gk_ref,
                    u_ref, q_ref, k_ref, v_ref, h_scr, *, d, ssm_w, sb_w):
    L = SSM_CHUNK
    r = x_ref.shape[0]
    for t in range(L):
        h_scr[t * r:(t + 1) * r, :] = _rms(x_ref[:, t * d:(t + 1) * d], g_ref[...]).astype(BF16)
    proj = jnp.dot(h_scr[...], w_ref[...], preferred_element_type=F32)
    u = proj[:, :ssm_w].astype(BF16)
    q = proj[:, ssm_w:ssm_w + sb_w]
    k = proj[:, ssm_w + sb_w:ssm_w + 2 * sb_w]
    v = proj[:, ssm_w + 2 * sb_w:].astype(BF16)
    seg = seg_ref[...]

    def head_norm(a, g):
        ms = jnp.dot((a * a).astype(BF16), seg, preferred_element_type=F32)
        return (a * lax.rsqrt(ms + NORM_EPS) * g).astype(BF16)

    q = head_norm(q, gq_ref[...])
    k = head_norm(k, gk_ref[...])
    for t in range(L):
        rows = slice(t * r, (t + 1) * r)
        q_ref[:, t * sb_w:(t + 1) * sb_w] = q[rows]
        k_ref[:, t * sb_w:(t + 1) * sb_w] = k[rows]
        v_ref[:, t * sb_w:(t + 1) * sb_w] = v[rows]
        for c in range(ssm_w // LANES):
            u_ref[c, :, t * LANES:(t + 1) * LANES] = u[rows, c * LANES:(c + 1) * LANES]


def _in_proj(x2, g_mix, w_in, g_q, g_k, ssm_w, sb_w):
    n, d = x2.shape
    L = SSM_CHUNK
    nr = n // L
    r = _token_tile(nr, ROWS_PER_STEP)
    nt = ssm_w // LANES
    heads = sb_w // SB_HEAD_DIM
    lane_head = jnp.arange(sb_w) // SB_HEAD_DIM
    seg = (lane_head[:, None] == lane_head[None, :]).astype(F32) / SB_HEAD_DIM
    gq = (jnp.tile(g_q.astype(F32), heads) * SB_Q_SCALE)[None]
    gk = jnp.tile(g_k.astype(F32), heads)[None]
    kern = functools.partial(_in_proj_kernel, d=d, ssm_w=ssm_w, sb_w=sb_w)
    row = lambda w: pl.BlockSpec((r, L * w), lambda i: (i, 0))
    u, q, k, v = pl.pallas_call(
        kern,
        grid=(nr // r,),
        in_specs=[row(d), _const_spec((1, d)), _const_spec(w_in.shape), _const_spec((sb_w, sb_w)),
                  _const_spec((1, sb_w)), _const_spec((1, sb_w))],
        out_specs=[pl.BlockSpec((nt, r, L * LANES), lambda i: (0, i, 0)), row(sb_w), row(sb_w), row(sb_w)],
        out_shape=[jax.ShapeDtypeStruct((nt, nr, L * LANES), BF16)]
        + [jax.ShapeDtypeStruct((nr, L * sb_w), BF16)] * 3,
        scratch_shapes=[pltpu.VMEM((L * r, d), BF16)],
        compiler_params=_params("parallel"),
        name="in_proj",
    )(x2.reshape(nr, L * d), g_mix.astype(F32)[None], w_in.astype(BF16), seg.astype(BF16), gq, gk)
    return u, q.reshape(n, sb_w), k.reshape(n, sb_w), v.reshape(n, sb_w)


def _s5_mats(a_re, a_im, log_dt, b_re, b_im, c_re, c_im):
    hp = lax.Precision.HIGHEST
    L = SSM_CHUNK
    g, p = a_re.shape
    nc = b_re.shape[-1]
    lr, li = a_re.astype(F32), a_im.astype(F32)
    dt = jnp.exp(log_dt.astype(F32))[:, None]
    n = jnp.arange(L + 1, dtype=F32)[:, None, None]
    mag = jnp.exp(n * (lr * dt)[None])
    pr, pi = mag * jnp.cos(n * (li * dt)[None]), mag * jnp.sin(n * (li * dt)[None])
    xr, xi = pr[1] - 1.0, pi[1]
    den = lr * lr + li * li
    fr, fi = (xr * lr + xi * li) / den, (xi * lr - xr * li) / den
    br, bi = b_re.astype(F32), b_im.astype(F32)
    bbr = fr[..., None] * br - fi[..., None] * bi
    bbi = fr[..., None] * bi + fi[..., None] * br
    cr, ci = c_re.astype(F32), c_im.astype(F32)
    mr = pr[..., None] * bbr[None] - pi[..., None] * bbi[None]
    mi = pr[..., None] * bbi[None] + pi[..., None] * bbr[None]
    kern = (jnp.einsum('gcp,ngpd->ngcd', cr, mr[:L], precision=hp)
            - jnp.einsum('gcp,ngpd->ngcd', ci, mi[:L], precision=hp))
    j = jnp.arange(L)[:, None]
    i = jnp.arange(L)[None, :]
    toe = jnp.where((i >= j)[:, :, None, None, None], kern[jnp.clip(i - j, 0, L - 1)], 0.0)
    t_mat = toe.transpose(2, 0, 4, 1, 3).reshape(g, L * nc, L * nc)
    rev = L - 1 - jnp.arange(L)
    pcr = mr[rev].transpose(1, 0, 3, 2).reshape(g, L * nc, p)
    pci = mi[rev].transpose(1, 0, 3, 2).reshape(g, L * nc, p)
    p_mat = jnp.concatenate([pcr, pci, pci, pcr], axis=-1)
    qr = cr[None] * pr[1:, :, None, :] - ci[None] * pi[1:, :, None, :]
    qi = cr[None] * pi[1:, :, None, :] + ci[None] * pr[1:, :, None, :]
    qr = qr.transpose(1, 3, 0, 2).reshape(g, p, L * nc)
    qi = qi.transpose(1, 3, 0, 2).reshape(g, p, L * nc)
    q_mat = jnp.concatenate([qr, -qi], axis=1)
    coef = jnp.stack([jnp.concatenate([pr[L], pr[L]], -1),
                      jnp.concatenate([-pi[L], pi[L]], -1)], axis=1)
    gt = LANES // nc
    nt = g // gt
    eye = jnp.eye(gt, dtype=F32)
    t6 = t_mat.reshape(nt, gt, L, nc, L, nc)
    tt = (t6[:, :, :, :, :, None, :] * eye[None, :, None, None, None, :, None])
    tt = tt.transpose(0, 2, 1, 3, 4, 5, 6).reshape(nt, L * LANES, L * LANES)
    p6 = p_mat.reshape(nt, gt, L, nc, 2, 2 * p)
    pp = (p6[:, :, :, :, :, None, :] * eye[None, :, None, None, None, :, None])
    pp = pp.transpose(0, 2, 1, 3, 4, 5, 6).reshape(nt, L * LANES, 4 * p * gt)
    q5 = q_mat.reshape(nt, gt, 2 * p, L, nc)
    qq = (q5[:, :, :, :, None, :] * eye[None, :, None, None, :, None])
    qq = qq.reshape(nt, 2 * p * gt, L * LANES)
    cf = coef.reshape(nt, gt, 2, 2 * p).transpose(0, 2, 1, 3).reshape(nt, 2, 2 * p * gt)
    return tt.astype(BF16), pp.astype(BF16), qq.astype(BF16), cf.astype(F32)


def _s5_kernel(u_ref, t_ref, p_ref, q_ref, coef_ref, d_ref, y_ref, e_scr, s_scr, *, sw):
    u = u_ref[0]
    nk = u.shape[0]
    e_scr[...] = jnp.dot(u, p_ref[0], preferred_element_type=F32)
    c1 = coef_ref[0, 0:1, :]
    c2 = coef_ref[0, 1:2, :]

    def step(k, carry):
        s, w = carry
        s_scr[pl.ds(k, 1), :] = s
        e = e_scr[pl.ds(k, 1), :]
        return s * c1 + w * c2 + e[:, :sw], w * c1 - s * c2 + e[:, sw:]

    zero = jnp.zeros((1, sw), F32)
    lax.fori_loop(0, nk, step, (zero, zero), unroll=8)
    y = jnp.dot(u, t_ref[0], preferred_element_type=F32)
    y += jnp.dot(s_scr[...].astype(BF16), q_ref[0], preferred_element_type=F32)
    y_ref[0] = y + d_ref[0] * u.astype(F32)


def _s5(u, mats, d_skip, bsz):
    tt, pp, qq, cf = mats
    nt, nr, lw = u.shape
    nk = nr // bsz
    sw = cf.shape[-1]
    d_row = jnp.tile(d_skip.astype(F32).reshape(nt, 1, LANES), (1, SSM_CHUNK, 1)).reshape(nt, 1, lw)
    kern = functools.partial(_s5_kernel, sw=sw)
    per_t = lambda a, b: pl.BlockSpec((1, a, b), lambda t, s: (t, 0, 0), pipeline_mode=pl.Buffered(1))
    seq_blk = pl.BlockSpec((1, nk, lw), lambda t, s: (t, s, 0))
    return pl.pallas_call(
        kern,
        grid=(nt, bsz),
        in_specs=[seq_blk, per_t(lw, lw), per_t(lw, 2 * sw), per_t(sw, lw), per_t(2, sw), per_t(1, lw)],
        out_specs=seq_blk,
        out_shape=jax.ShapeDtypeStruct((nt, nr, lw), F32),
        scratch_shapes=[pltpu.VMEM((nk, 2 * sw), F32), pltpu.VMEM((nk, sw), F32)],
        compiler_params=_params("parallel", "parallel"),
        name="s5_scan",
    )(u, tt, pp, qq, cf, d_row)


F32_ZERO_EXP2 = 160.0


def _sb_attn_kernel(thr_ref, q_ref, k_ref, v_ref, tri_ref, o_ref, *, blk):
    i = pl.program_id(2)
    q = q_ref[0]
    lane = lax.broadcasted_iota(jnp.int32, q.shape, 1)
    first = lane < SB_HEAD_DIM
    zq = jnp.zeros_like(q)
    qh = (jnp.where(first, q, zq), jnp.where(first, zq, q))
    tri = tri_ref[...]
    row = lax.broadcasted_iota(jnp.int32, (blk, blk), 0)
    col = lax.broadcasted_iota(jnp.int32, (blk, blk), 1)
    below = col < row

    def tile(qv, kb, vb, carry, acc, mask):
        z = lax.dot_general(qv, kb, (((1,), (1,)), ((), ())), preferred_element_type=F32)
        sp = jnp.maximum(z, 0.0) + jnp.log(1.0 + jnp.exp2(-jnp.abs(z))) * LOG2E
        if mask is not None:
            sp = jnp.where(mask, sp, 0.0)
        hi = sp.astype(BF16)
        lo = (sp - hi.astype(F32)).astype(BF16)
        cum = jnp.dot(jnp.concatenate([hi, lo], axis=1), tri, preferred_element_type=F32)
        w = jnp.exp2(z - cum - carry)
        if mask is not None:
            w = jnp.where(mask, w, 0.0)
        acc = acc + jnp.dot(w.astype(BF16), vb, preferred_element_type=F32)
        return carry + cum[:, 0:1], acc

    def block(j, state, mask):
        start = pl.multiple_of(j * blk, blk)
        kb = k_ref[0, pl.ds(start, blk), :]
        vb = v_ref[0, pl.ds(start, blk), :]
        c0, a0, c1, a1 = state
        c0, a0 = tile(qh[0], kb, vb, c0, a0, mask)
        c1, a1 = tile(qh[1], kb, vb, c1, a1, mask)
        return c0, a0, c1, a1

    zc = jnp.zeros((blk, 1), F32)
    za = jnp.zeros((blk, LANES), F32)
    state = block(i, (zc, za, zc, za), below)
    thr = thr_ref[0]

    def more(st):
        return jnp.logical_and(st[0] >= 0, st[1] <= thr)

    def step(st):
        j = st[0]
        new = block(j, st[2:], None)
        return (j - 1, jnp.min(jnp.minimum(new[0], new[2]))) + new

    state = lax.while_loop(more, step, (i - 1, jnp.float32(0.0)) + state)
    o_ref[0] = jnp.where(first, state[3], state[5]).astype(o_ref.dtype)


def _sb_attn(q, k, v, z_bound):
    bsz, seq, width = q.shape
    blk = _token_tile(seq, 256)
    assert width % LANES == 0
    tri = (jnp.arange(blk)[:, None] >= jnp.arange(blk)[None, :]).astype(BF16)
    tri = jnp.concatenate([tri, tri], axis=0)
    thr = (z_bound + F32_ZERO_EXP2).astype(F32).reshape(1)
    kern = functools.partial(_sb_attn_kernel, blk=blk)
    return pl.pallas_call(
        kern,
        grid=(bsz, width // LANES, seq // blk),
        in_specs=[pl.BlockSpec(memory_space=pltpu.SMEM),
                  pl.BlockSpec((1, blk, LANES), lambda b, h, i: (b, i, h)),
                  pl.BlockSpec((1, seq, LANES), lambda b, h, i: (b, 0, h)),
                  pl.BlockSpec((1, seq, LANES), lambda b, h, i: (b, 0, h)),
                  _const_spec((2 * blk, blk))],
        out_specs=pl.BlockSpec((1, blk, LANES), lambda b, h, i: (b, i, h)),
        out_shape=jax.ShapeDtypeStruct((bsz, seq, width), BF16),
        compiler_params=_params("parallel", "parallel", "arbitrary"),
        name="sb_attn",
    )(thr, q, k, v, tri)


def _mix_out_kernel(x_ref, ys_ref, sb_ref, wg_ref, gs_ref, gb_ref, wo_ref, o_ref, y_scr, sb_scr,
                    *, d, ssm_w, sb_w):
    L = SSM_CHUNK
    r = x_ref.shape[0]
    for t in range(L):
        rows = slice(t * r, (t + 1) * r)
        for c in range(ssm_w // LANES):
            y_scr[rows, c * LANES:(c + 1) * LANES] = ys_ref[c, :, t * LANES:(t + 1) * LANES]
        sb_scr[rows, :] = sb_ref[:, t * sb_w:(t + 1) * sb_w]
    y = jax.nn.gelu(y_scr[...])
    gate = jnp.dot(y.astype(BF16), wg_ref[...], preferred_element_type=F32)
    y = y * jax.nn.sigmoid(gate)
    yn = _rms(y, gs_ref[...]).astype(BF16)
    sn = _rms(sb_scr[...].astype(F32), gb_ref[...]).astype(BF16)
    out = jnp.dot(yn, wo_ref[:ssm_w, :], preferred_element_type=F32)
    out += jnp.dot(sn, wo_ref[ssm_w:, :], preferred_element_type=F32)
    for t in range(L):
        o_ref[:, t * d:(t + 1) * d] = x_ref[:, t * d:(t + 1) * d] + out[t * r:(t + 1) * r]


def _mix_out(x2, y_s5, y_sb, w_glu, g_ssm, g_sb, w_out):
    n, d = x2.shape
    L = SSM_CHUNK
    nt, nr, lw = y_s5.shape
    ssm_w, sb_w = nt * LANES, y_sb.shape[1]
    r = _token_tile(nr, ROWS_PER_STEP)
    kern = functools.partial(_mix_out_kernel, d=d, ssm_w=ssm_w, sb_w=sb_w)
    row = lambda w: pl.BlockSpec((r, L * w), lambda i: (i, 0))
    out = pl.pallas_call(
        kern,
        grid=(nr // r,),
        in_specs=[row(d), pl.BlockSpec((nt, r, lw), lambda i: (0, i, 0)), row(sb_w),
                  _const_spec(w_glu.shape), _const_spec((1, ssm_w)), _const_spec((1, sb_w)),
                  _const_spec(w_out.shape)],
        out_specs=row(d),
        out_shape=jax.ShapeDtypeStruct((nr, L * d), F32),
        scratch_shapes=[pltpu.VMEM((L * r, ssm_w), F32), pltpu.VMEM((L * r, sb_w), BF16)],
        compiler_params=_params("parallel"),
        name="mix_out",
    )(x2.reshape(nr, L * d), y_s5, y_sb.reshape(nr, L * sb_w), w_glu.astype(BF16),
      g_ssm.astype(F32)[None], g_sb.astype(F32)[None], w_out.astype(BF16))
    return out.reshape(n, d)


def _mem_kv_kernel(m_ref, g_ref, w_ref, gk_ref, k_ref, v_ref, *, xa_w):
    h = _rms(m_ref[0], g_ref[...])
    kv = jnp.dot(h.astype(BF16), w_ref[...], preferred_element_type=F32)
    for hd in range(xa_w // XA_HEAD_DIM):
        sl = slice(hd * XA_HEAD_DIM, (hd + 1) * XA_HEAD_DIM)
        k_ref[0, :, sl] = _rms(kv[:, sl], gk_ref[...]).astype(BF16)
    v_ref[0] = kv[:, xa_w:].astype(BF16)


def _mem_kv(mem, g_mem, w_kv, g_k):
    bsz, mlen, d = mem.shape
    xa_w = w_kv.shape[1] // 2
    kern = functools.partial(_mem_kv_kernel, xa_w=xa_w)
    per_b = lambda w: pl.BlockSpec((1, mlen, w), lambda b: (b, 0, 0))
    return pl.pallas_call(
        kern,
        grid=(bsz,),
        in_specs=[per_b(d), _const_spec((1, d)), _const_spec(w_kv.shape), _const_spec((1, XA_HEAD_DIM))],
        out_specs=[per_b(xa_w), per_b(xa_w)],
        out_shape=[jax.ShapeDtypeStruct((bsz, mlen, xa_w), BF16)] * 2,
        compiler_params=_params("parallel"),
        name="mem_kv",
    )(mem, g_mem.astype(F32)[None], w_kv.astype(BF16), g_k.astype(F32)[None])


def _xattn_kernel(x_ref, g_ref, wq_ref, gq_ref, k_ref, v_ref, wo_ref, o_ref):
    x = x_ref[0]
    h = _rms(x, g_ref[...])
    q = jnp.dot(h.astype(BF16), wq_ref[...], preferred_element_type=F32)
    heads = []
    for hd in range(q.shape[1] // XA_HEAD_DIM):
        sl = slice(hd * XA_HEAD_DIM, (hd + 1) * XA_HEAD_DIM)
        qh = _rms(q[:, sl], gq_ref[...]).astype(BF16)
        s = lax.dot_general(qh, k_ref[0, :, sl], (((1,), (1,)), ((), ())), preferred_element_type=F32)
        e = jnp.exp(s - jnp.max(s, axis=-1, keepdims=True))
        pv = jnp.dot(e.astype(BF16), v_ref[0, :, sl], preferred_element_type=F32)
        heads.append((pv / jnp.sum(e, axis=-1, keepdims=True)).astype(BF16))
    o = jnp.concatenate(heads, axis=-1)
    o_ref[0] = x + jnp.dot(o, wo_ref[...], preferred_element_type=F32)


def _xattn(x, g_xa, w_q, g_q, k, v, w_o):
    bsz, seq, d = x.shape
    mlen, xa_w = k.shape[1], k.shape[2]
    tm = _token_tile(seq, 512)
    gq = (g_q.astype(F32) * (XA_HEAD_DIM ** -0.5))[None]
    return pl.pallas_call(
        _xattn_kernel,
        grid=(bsz, seq // tm),
        in_specs=[pl.BlockSpec((1, tm, d), lambda b, i: (b, i, 0)), _const_spec((1, d)),
                  _const_spec(w_q.shape), _const_spec((1, XA_HEAD_DIM)),
                  pl.BlockSpec((1, mlen, xa_w), lambda b, i: (b, 0, 0)),
                  pl.BlockSpec((1, mlen, xa_w), lambda b, i: (b, 0, 0)),
                  _const_spec(w_o.shape)],
        out_specs=pl.BlockSpec((1, tm, d), lambda b, i: (b, i, 0)),
        out_shape=jax.ShapeDtypeStruct((bsz, seq, d), F32),
        compiler_params=_params("parallel", "parallel"),
        name="xattn",
    )(x, g_xa.astype(F32)[None], w_q.astype(BF16), gq, k, v, w_o.astype(BF16))


def _mlp_kernel(x_ref, g_ref, wu_ref, wd_ref, o_ref, *, ff_chunk):
    x = x_ref[...]
    h = _rms(x, g_ref[...]).astype(BF16)
    acc = x
    for c in range(wu_ref.shape[1] // ff_chunk):
        sl = slice(c * ff_chunk, (c + 1) * ff_chunk)
        a = jnp.maximum(jnp.dot(h, wu_ref[:, sl], preferred_element_type=F32), 0.0)
        acc = acc + jnp.dot((a * a).astype(BF16), wd_ref[sl, :], preferred_element_type=F32)
    o_ref[...] = acc


def _mlp(x2, g_mlp, w_up, w_down):
    n, d = x2.shape
    tm = _token_tile(n, 512)
    ff_chunk = min(w_up.shape[1], 1024)
    kern = functools.partial(_mlp_kernel, ff_chunk=ff_chunk)
    tok = pl.BlockSpec((tm, d), lambda i: (i, 0))
    return pl.pallas_call(
        kern,
        grid=(n // tm,),
        in_specs=[tok, _const_spec((1, d)), _const_spec(w_up.shape), _const_spec(w_down.shape)],
        out_specs=tok,
        out_shape=jax.ShapeDtypeStruct((n, d), F32),
        compiler_params=_params("parallel"),
        name="mlp",
    )(x2, g_mlp.astype(F32)[None], w_up.astype(BF16), w_down.astype(BF16))


def kernel(x, mem, g_mix, w_in, ssm_a_re, ssm_a_im, ssm_log_dt, ssm_b_re, ssm_b_im, ssm_c_re, ssm_c_im,
           ssm_d, ssm_w_glu, sb_g_q, sb_g_k, g_out_ssm, g_out_sb, w_out, g_xa, g_mem, xa_w_q, xa_w_kv,
           xa_g_q, xa_g_k, xa_w_o, g_mlp, w_up, w_down):
    bsz, seq, d = x.shape
    n = bsz * seq
    ssm_w = ssm_d.shape[-1]
    sb_w = g_out_sb.shape[-1]
    x = x.astype(F32)
    for l in range(g_mix.shape[0]):
        u, q, k, v = _in_proj(x.reshape(n, d), g_mix[l], w_in[l], sb_g_q[l], sb_g_k[l], ssm_w, sb_w)
        mats = _s5_mats(ssm_a_re[l], ssm_a_im[l], ssm_log_dt[l], ssm_b_re[l], ssm_b_im[l],
                        ssm_c_re[l], ssm_c_im[l])
        y_s5 = _s5(u, mats, ssm_d[l], bsz)
        z_bound = (SB_HEAD_DIM * SB_Q_SCALE * SB_BOUND_SLACK
                   * jnp.max(jnp.abs(sb_g_q[l].astype(F32))) * jnp.max(jnp.abs(sb_g_k[l].astype(F32))))
        y_sb = _sb_attn(q.reshape(bsz, seq, sb_w), k.reshape(bsz, seq, sb_w), v.reshape(bsz, seq, sb_w), z_bound)
        x1 = _mix_out(x.reshape(n, d), y_s5, y_sb.reshape(n, sb_w), ssm_w_glu[l],
                      g_out_ssm[l], g_out_sb[l], w_out[l])
        mk, mv = _mem_kv(mem.astype(F32), g_mem[l], xa_w_kv[l], xa_g_k[l])
        x2 = _xattn(x1.reshape(bsz, seq, d), g_xa[l], xa_w_q[l], xa_g_q[l], mk, mv, xa_w_o[l])
        x = _mlp(x2.reshape(n, d), g_mlp[l], w_up[l], w_down[l]).reshape(bsz, seq, d)
    return x
```

```python
import functools

import jax
import jax.numpy as jnp
from jax import lax
from jax.experimental import pallas as pl
from jax.experimental.pallas import tpu as pltpu

F32 = jnp.float32
BF16 = jnp.bfloat16

NORM_EPS = 1e-6
SSM_GROUP = 16
SSM_STATE = 64
SSM_CHUNK = 16
SB_HEAD_DIM = 64
XA_HEADS = 4
XA_HEAD_DIM = 128
LANES = 128
LOG2E = 1.4426950408889634
SB_Q_SCALE = SB_HEAD_DIM ** -0.5 * LOG2E
SB_BOUND_SLACK = 1.05
VMEM_LIMIT_BYTES = 56 * 1024 * 1024


def _params(*sem):
    return pltpu.CompilerParams(dimension_semantics=sem, vmem_limit_bytes=VMEM_LIMIT_BYTES)


def _const_spec(shape):
    zeros = (0,) * len(shape)
    return pl.BlockSpec(shape, lambda *_: zeros, pipeline_mode=pl.Buffered(1))


def _rms(x, g):
    return x * lax.rsqrt(jnp.mean(x * x, axis=-1, keepdims=True) + NORM_EPS) * g


def _token_tile(n, want):
    t = min(n, want)
    assert n % t == 0
    return t


def _in_proj_kernel(x_ref, g_ref, w_ref, seg_ref, gq_ref, gk_ref,
                    u_ref, q_ref, k_ref, v_ref, *, ssm_w, sb_w):
    h = _rms(x_ref[...], g_ref[...])
    proj = jnp.dot(h.astype(BF16), w_ref[...], preferred_element_type=F32)
    u_ref[...] = proj[:, :ssm_w]
    q = proj[:, ssm_w:ssm_w + sb_w]
    k = proj[:, ssm_w + sb_w:ssm_w + 2 * sb_w]
    v_ref[...] = proj[:, ssm_w + 2 * sb_w:].astype(BF16)
    seg = seg_ref[...]

    def head_norm(t, g):
        ms = jnp.dot((t * t).astype(BF16), seg, preferred_element_type=F32)
        return t * lax.rsqrt(ms + NORM_EPS) * g

    q_ref[...] = head_norm(q, gq_ref[...]).astype(BF16)
    k_ref[...] = head_norm(k, gk_ref[...]).astype(BF16)


def _in_proj(x2, g_mix, w_in, g_q, g_k, ssm_w, sb_w):
    n, d = x2.shape
    tm = _token_tile(n, 512)
    heads = sb_w // SB_HEAD_DIM
    lane_head = jnp.arange(sb_w) // SB_HEAD_DIM
    seg = (lane_head[:, None] == lane_head[None, :]).astype(F32) / SB_HEAD_DIM
    gq = (jnp.tile(g_q.astype(F32), heads) * SB_Q_SCALE)[None]
    gk = jnp.tile(g_k.astype(F32), heads)[None]
    kern = functools.partial(_in_proj_kernel, ssm_w=ssm_w, sb_w=sb_w)
    tok = lambda w: pl.BlockSpec((tm, w), lambda i: (i, 0))
    return pl.pallas_call(
        kern,
        grid=(n // tm,),
        in_specs=[tok(d), _const_spec((1, d)), _const_spec(w_in.shape), _const_spec((sb_w, sb_w)),
                  _const_spec((1, sb_w)), _const_spec((1, sb_w))],
        out_specs=[tok(ssm_w), tok(sb_w), tok(sb_w), tok(sb_w)],
        out_shape=[jax.ShapeDtypeStruct((n, ssm_w), F32)] + [jax.ShapeDtypeStruct((n, sb_w), BF16)] * 3,
        compiler_params=_params("parallel"),
        name="in_proj",
    )(x2, g_mix.astype(F32)[None], w_in.astype(BF16), seg.astype(BF16), gq, gk)


def _s5_mats(a_re, a_im, log_dt, b_re, b_im, c_re, c_im):
    hp = lax.Precision.HIGHEST
    L = SSM_CHUNK
    g, p = a_re.shape
    nc = b_re.shape[-1]
    lr, li = a_re.astype(F32), a_im.astype(F32)
    dt = jnp.exp(log_dt.astype(F32))[:, None]
    n = jnp.arange(L + 1, dtype=F32)[:, None, None]
    mag = jnp.exp(n * (lr * dt)[None])
    pr, pi = mag * jnp.cos(n * (li * dt)[None]), mag * jnp.sin(n * (li * dt)[None])
    xr, xi = pr[1] - 1.0, pi[1]
    den = lr * lr + li * li
    fr, fi = (xr * lr + xi * li) / den, (xi * lr - xr * li) / den
    br, bi = b_re.astype(F32), b_im.astype(F32)
    bbr = fr[..., None] * br - fi[..., None] * bi
    bbi = fr[..., None] * bi + fi[..., None] * br
    cr, ci = c_re.astype(F32), c_im.astype(F32)
    mr = pr[..., None] * bbr[None] - pi[..., None] * bbi[None]
    mi = pr[..., None] * bbi[None] + pi[..., None] * bbr[None]
    kern = (jnp.einsum('gcp,ngpd->ngcd', cr, mr[:L], precision=hp)
            - jnp.einsum('gcp,ngpd->ngcd', ci, mi[:L], precision=hp))
    rev = L - 1 - jnp.arange(L)
    pcr = mr[rev].transpose(1, 0, 3, 2).reshape(g, L * nc, p)
    pci = mi[rev].transpose(1, 0, 3, 2).reshape(g, L * nc, p)
    p_mat = jnp.concatenate([pcr, pci, pci, pcr], axis=-1)
    qr = cr[None] * pr[1:, :, None, :] - ci[None] * pi[1:, :, None, :]
    qi = cr[None] * pi[1:, :, None, :] + ci[None] * pr[1:, :, None, :]
    qr = qr.transpose(1, 3, 0, 2).reshape(g, p, L * nc)
    qi = qi.transpose(1, 3, 0, 2).reshape(g, p, L * nc)
    q_mat = jnp.concatenate([qr, -qi], axis=1)
    coef = jnp.stack([jnp.concatenate([pr[L], pr[L]], -1),
                      jnp.concatenate([-pi[L], pi[L]], -1)], axis=1)
    gt = LANES // nc
    nt = g // gt
    eye = jnp.eye(gt, dtype=F32)
    k5 = kern.reshape(L, nt, gt, nc, nc).transpose(1, 0, 2, 4, 3)
    bdk = k5[:, :, :, :, None, :] * eye[None, None, :, None, :, None]
    bdk = bdk.reshape(nt, L * LANES, LANES)
    pc = p_mat.reshape(nt, gt, L, nc, 4 * p).transpose(0, 2, 1, 3, 4).reshape(nt, L * LANES, 4 * p)
    qc = q_mat.reshape(nt, gt * 2 * p, L * nc)
    cf = coef.reshape(nt, gt, 2, 2 * p).transpose(0, 2, 1, 3).reshape(nt, 2, 2 * p * gt)
    col = jnp.arange(L * LANES)
    spread = (jnp.arange(L * nc)[:, None] == (col // LANES * nc + col % nc)[None, :])
    return bdk.astype(BF16), pc.astype(BF16), qc.astype(BF16), spread.astype(BF16), cf.astype(F32)


def _s5_kernel(u_ref, bdk_ref, pc_ref, qc_ref, spread_ref, coef_ref, d_ref, y_ref,
               tt_scr, pp_scr, qq_scr, e_scr, s_scr, *, sw, nc):
    L = SSM_CHUNK
    gt = LANES // nc

    @pl.when(pl.program_id(1) == 0)
    def _expand_tile_matrices():
        zero = jnp.zeros((LANES, LANES), BF16)
        for j in range(L):
            for i in range(L):
                lag = (i - j) * LANES
                tt_scr[j * LANES:(j + 1) * LANES, i * LANES:(i + 1) * LANES] = (
                    bdk_ref[0, lag:lag + LANES, :] if i >= j else zero)
        pc = pc_ref[0].astype(F32)
        row_g = lax.broadcasted_iota(jnp.int32, (L * LANES, LANES), 0) // nc % gt
        for half in range(2):
            src = pc[:, half * LANES:(half + 1) * LANES]
            for g in range(gt):
                lo = half * sw + g * LANES
                pp_scr[:, lo:lo + LANES] = jnp.where(row_g == g, src, 0.0).astype(BF16)
        full = jnp.dot(qc_ref[0], spread_ref[...], preferred_element_type=F32)
        rg = lax.broadcasted_iota(jnp.int32, full.shape, 0) // LANES
        cg = lax.broadcasted_iota(jnp.int32, full.shape, 1) // nc % gt
        qq_scr[...] = jnp.where(rg == cg, full, 0.0).astype(BF16)

    nk = u_ref.shape[0]
    u = jnp.concatenate([u_ref[:, t, :] for t in range(L)], axis=1).astype(BF16)
    e_scr[...] = jnp.dot(u, pp_scr[...], preferred_element_type=F32)
    c1 = coef_ref[0, 0:1, :]
    c2 = coef_ref[0, 1:2, :]

    def step(k, carry):
        s, w = carry
        s_scr[pl.ds(k, 1), :] = s
        e = e_scr[pl.ds(k, 1), :]
        return s * c1 + w * c2 + e[:, :sw], w * c1 - s * c2 + e[:, sw:]

    zero = jnp.zeros((1, sw), F32)
    lax.fori_loop(0, nk, step, (zero, zero), unroll=8)
    y = jnp.dot(u, tt_scr[...], preferred_element_type=F32)
    y += jnp.dot(s_scr[...].astype(BF16), qq_scr[...], preferred_element_type=F32)
    for i in range(L):
        y_ref[:, i, :] = y[:, i * LANES:(i + 1) * LANES] + d_ref[0] * u_ref[:, i, :]


def _s5(u, mats, d_skip, bsz):
    bdk, pc, qc, spread, cf = mats
    n, width = u.shape
    L = SSM_CHUNK
    nt = width // LANES
    nk = n // L // bsz
    lw, sw = L * LANES, cf.shape[-1]
    nc = qc.shape[-1] // L
    kern = functools.partial(_s5_kernel, sw=sw, nc=nc)
    per_t = lambda a, b: pl.BlockSpec((1, a, b), lambda t, s: (t, 0, 0))
    seq_blk = pl.BlockSpec((nk, L, LANES), lambda t, s: (s, 0, t))
    y = pl.pallas_call(
        kern,
        grid=(nt, bsz),
        in_specs=[seq_blk, per_t(lw, LANES), per_t(lw, 2 * LANES), per_t(sw, L * nc),
                  _const_spec(spread.shape), per_t(2, sw), per_t(1, LANES)],
        out_specs=seq_blk,
        out_shape=jax.ShapeDtypeStruct((n // L, L, width), F32),
        scratch_shapes=[pltpu.VMEM((lw, lw), BF16), pltpu.VMEM((lw, 2 * sw), BF16), pltpu.VMEM((sw, lw), BF16),
                        pltpu.VMEM((nk, 2 * sw), F32), pltpu.VMEM((nk, sw), F32)],
        compiler_params=_params("parallel", "arbitrary"),
        name="s5_scan",
    )(u.reshape(n // L, L, width), bdk, pc, qc, spread, cf, d_skip.astype(F32).reshape(nt, 1, LANES))
    return y.reshape(n, width)


F32_ZERO_EXP2 = 160.0


def _sb_attn_kernel(thr_ref, q_ref, k_ref, v_ref, tri_ref, o_ref, *, blk):
    i = pl.program_id(2)
    q = q_ref[0]
    lane = lax.broadcasted_iota(jnp.int32, q.shape, 1)
    first = lane < SB_HEAD_DIM
    zq = jnp.zeros_like(q)
    qh = (jnp.where(first, q, zq), jnp.where(first, zq, q))
    tri = tri_ref[...]
    row = lax.broadcasted_iota(jnp.int32, (blk, blk), 0)
    col = lax.broadcasted_iota(jnp.int32, (blk, blk), 1)
    below = col < row

    def tile(qv, kb, vb, carry, acc, mask):
        z = lax.dot_general(qv, kb, (((1,), (1,)), ((), ())), preferred_element_type=F32)
        sp = jnp.maximum(z, 0.0) + jnp.log(1.0 + jnp.exp2(-jnp.abs(z))) * LOG2E
        if mask is not None:
            sp = jnp.where(mask, sp, 0.0)
        hi = sp.astype(BF16)
        lo = (sp - hi.astype(F32)).astype(BF16)
        cum = jnp.dot(jnp.concatenate([hi, lo], axis=1), tri, preferred_element_type=F32)
        w = jnp.exp2(z - cum - carry)
        if mask is not None:
            w = jnp.where(mask, w, 0.0)
        acc = acc + jnp.dot(w.astype(BF16), vb, preferred_element_type=F32)
        return carry + cum[:, 0:1], acc

    def block(j, state, mask):
        start = pl.multiple_of(j * blk, blk)
        kb = k_ref[0, pl.ds(start, blk), :]
        vb = v_ref[0, pl.ds(start, blk), :]
        c0, a0, c1, a1 = state
        c0, a0 = tile(qh[0], kb, vb, c0, a0, mask)
        c1, a1 = tile(qh[1], kb, vb, c1, a1, mask)
        return c0, a0, c1, a1

    zc = jnp.zeros((blk, 1), F32)
    za = jnp.zeros((blk, LANES), F32)
    state = block(i, (zc, za, zc, za), below)
    thr = thr_ref[0]

    def more(st):
        return jnp.logical_and(st[0] >= 0, st[1] <= thr)

    def step(st):
        j = st[0]
        new = block(j, st[2:], None)
        return (j - 1, jnp.min(jnp.minimum(new[0], new[2]))) + new

    state = lax.while_loop(more, step, (i - 1, jnp.float32(0.0)) + state)
    o_ref[0] = jnp.where(first, state[3], state[5]).astype(o_ref.dtype)


def _sb_attn(q, k, v, z_bound):
    bsz, seq, width = q.shape
    blk = _token_tile(seq, 256)
    assert width % LANES == 0
    tri = (jnp.arange(blk)[:, None] >= jnp.arange(blk)[None, :]).astype(BF16)
    tri = jnp.concatenate([tri, tri], axis=0)
    thr = (z_bound + F32_ZERO_EXP2).astype(F32).reshape(1)
    kern = functools.partial(_sb_attn_kernel, blk=blk)
    return pl.pallas_call(
        kern,
        grid=(bsz, width // LANES, seq // blk),
        in_specs=[pl.BlockSpec(memory_space=pltpu.SMEM),
                  pl.BlockSpec((1, blk, LANES), lambda b, h, i: (b, i, h)),
                  pl.BlockSpec((1, seq, LANES), lambda b, h, i: (b, 0, h)),
                  pl.BlockSpec((1, seq, LANES), lambda b, h, i: (b, 0, h)),
                  _const_spec((2 * blk, blk))],
        out_specs=pl.BlockSpec((1, blk, LANES), lambda b, h, i: (b, i, h)),
        out_shape=jax.ShapeDtypeStruct((bsz, seq, width), BF16),
        compiler_params=_params("parallel", "parallel", "arbitrary"),
        name="sb_attn",
    )(thr, q, k, v, tri)


def _mix_out_kernel(x_ref, ys_ref, sb_ref, wg_ref, gs_ref, gb_ref, wo_ref, o_ref, *, ssm_w):
    y = jax.nn.gelu(ys_ref[...])
    gate = jnp.dot(y.astype(BF16), wg_ref[...], preferred_element_type=F32)
    y = y * jax.nn.sigmoid(gate)
    yn = _rms(y, gs_ref[...]).astype(BF16)
    sn = _rms(sb_ref[...].astype(F32), gb_ref[...]).astype(BF16)
    out = jnp.dot(yn, wo_ref[:ssm_w, :], preferred_element_type=F32)
    out += jnp.dot(sn, wo_ref[ssm_w:, :], preferred_element_type=F32)
    o_ref[...] = x_ref[...] + out


def _mix_out(x2, y_s5, y_sb, w_glu, g_ssm, g_sb, w_out):
    n, d = x2.shape
    ssm_w, sb_w = y_s5.shape[1], y_sb.shape[1]
    tm = _token_tile(n, 512)
    kern = functools.partial(_mix_out_kernel, ssm_w=ssm_w)
    tok = lambda w: pl.BlockSpec((tm, w), lambda i: (i, 0))
    return pl.pallas_call(
        kern,
        grid=(n // tm,),
        in_specs=[tok(d), tok(ssm_w), tok(sb_w), _const_spec(w_glu.shape), _const_spec((1, ssm_w)),
                  _const_spec((1, sb_w)), _const_spec(w_out.shape)],
        out_specs=tok(d),
        out_shape=jax.ShapeDtypeStruct((n, d), F32),
        compiler_params=_params("parallel"),
        name="mix_out",
    )(x2, y_s5, y_sb, w_glu.astype(BF16), g_ssm.astype(F32)[None], g_sb.astype(F32)[None], w_out.astype(BF16))


def _mem_kv_kernel(m_ref, g_ref, w_ref, gk_ref, k_ref, v_ref, *, xa_w):
    h = _rms(m_ref[0], g_ref[...])
    kv = jnp.dot(h.astype(BF16), w_ref[...], preferred_element_type=F32)
    for hd in range(xa_w // XA_HEAD_DIM):
        sl = slice(hd * XA_HEAD_DIM, (hd + 1) * XA_HEAD_DIM)
        k_ref[0, :, sl] = _rms(kv[:, sl], gk_ref[...]).astype(BF16)
    v_ref[0] = kv[:, xa_w:].astype(BF16)


def _mem_kv(mem, g_mem, w_kv, g_k):
    bsz, mlen, d = mem.shape
    xa_w = w_kv.shape[1] // 2
    kern = functools.partial(_mem_kv_kernel, xa_w=xa_w)
    per_b = lambda w: pl.BlockSpec((1, mlen, w), lambda b: (b, 0, 0))
    return pl.pallas_call(
        kern,
        grid=(bsz,),
        in_specs=[per_b(d), _const_spec((1, d)), _const_spec(w_kv.shape), _const_spec((1, XA_HEAD_DIM))],
        out_specs=[per_b(xa_w), per_b(xa_w)],
        out_shape=[jax.ShapeDtypeStruct((bsz, mlen, xa_w), BF16)] * 2,
        compiler_params=_params("parallel"),
        name="mem_kv",
    )(mem, g_mem.astype(F32)[None], w_kv.astype(BF16), g_k.astype(F32)[None])


def _xattn_kernel(x_ref, g_ref, wq_ref, gq_ref, k_ref, v_ref, wo_ref, o_ref):
    x = x_ref[0]
    h = _rms(x, g_ref[...])
    q = jnp.dot(h.astype(BF16), wq_ref[...], preferred_element_type=F32)
    heads = []
    for hd in range(q.shape[1] // XA_HEAD_DIM):
        sl = slice(hd * XA_HEAD_DIM, (hd + 1) * XA_HEAD_DIM)
        qh = _rms(q[:, sl], gq_ref[...]).astype(BF16)
        s = lax.dot_general(qh, k_ref[0, :, sl], (((1,), (1,)), ((), ())), preferred_element_type=F32)
        e = jnp.exp(s - jnp.max(s, axis=-1, keepdims=True))
        pv = jnp.dot(e.astype(BF16), v_ref[0, :, sl], preferred_element_type=F32)
        heads.append((pv / jnp.sum(e, axis=-1, keepdims=True)).astype(BF16))
    o = jnp.concatenate(heads, axis=-1)
    o_ref[0] = x + jnp.dot(o, wo_ref[...], preferred_element_type=F32)


def _xattn(x, g_xa, w_q, g_q, k, v, w_o):
    bsz, seq, d = x.shape
    mlen, xa_w = k.shape[1], k.shape[2]
    tm = _token_tile(seq, 512)
    gq = (g_q.astype(F32) * (XA_HEAD_DIM ** -0.5))[None]
    return pl.pallas_call(
        _xattn_kernel,
        grid=(bsz, seq // tm),
        in_specs=[pl.BlockSpec((1, tm, d), lambda b, i: (b, i, 0)), _const_spec((1, d)),
                  _const_spec(w_q.shape), _const_spec((1, XA_HEAD_DIM)),
                  pl.BlockSpec((1, mlen, xa_w), lambda b, i: (b, 0, 0)),
                  pl.BlockSpec((1, mlen, xa_w), lambda b, i: (b, 0, 0)),
                  _const_spec(w_o.shape)],
        out_specs=pl.BlockSpec((1, tm, d), lambda b, i: (b, i, 0)),
        out_shape=jax.ShapeDtypeStruct((bsz, seq, d), F32),
        compiler_params=_params("parallel", "parallel"),
        name="xattn",
    )(x, g_xa.astype(F32)[None], w_q.astype(BF16), gq, k, v, w_o.astype(BF16))


def _mlp_kernel(x_ref, g_ref, wu_ref, wd_ref, o_ref, *, ff_chunk):
    x = x_ref[...]
    h = _rms(x, g_ref[...]).astype(BF16)
    acc = x
    for c in range(wu_ref.shape[1] // ff_chunk):
        sl = slice(c * ff_chunk, (c + 1) * ff_chunk)
        a = jnp.maximum(jnp.dot(h, wu_ref[:, sl], preferred_element_type=F32), 0.0)
        acc = acc + jnp.dot((a * a).astype(BF16), wd_ref[sl, :], preferred_element_type=F32)
    o_ref[...] = acc


def _mlp(x2, g_mlp, w_up, w_down):
    n, d = x2.shape
    tm = _token_tile(n, 512)
    ff_chunk = min(w_up.shape[1], 1024)
    kern = functools.partial(_mlp_kernel, ff_chunk=ff_chunk)
    tok = pl.BlockSpec((tm, d), lambda i: (i, 0))
    return pl.pallas_call(
        kern,
        grid=(n // tm,),
        in_specs=[tok, _const_spec((1, d)), _const_spec(w_up.shape), _const_spec(w_down.shape)],
        out_specs=tok,
        out_shape=jax.ShapeDtypeStruct((n, d), F32),
        compiler_params=_params("parallel"),
        name="mlp",
    )(x2, g_mlp.astype(F32)[None], w_up.astype(BF16), w_down.astype(BF16))


def kernel(x, mem, g_mix, w_in, ssm_a_re, ssm_a_im, ssm_log_dt, ssm_b_re, ssm_b_im, ssm_c_re, ssm_c_im,
           ssm_d, ssm_w_glu, sb_g_q, sb_g_k, g_out_ssm, g_out_sb, w_out, g_xa, g_mem, xa_w_q, xa_w_kv,
           xa_g_q, xa_g_k, xa_w_o, g_mlp, w_up, w_down):
    bsz, seq, d = x.shape
    n = bsz * seq
    ssm_w = ssm_d.shape[-1]
    sb_w = g_out_sb.shape[-1]
    x = x.astype(F32)
    for l in range(g_mix.shape[0]):
        u, q, k, v = _in_proj(x.reshape(n, d), g_mix[l], w_in[l], sb_g_q[l], sb_g_k[l], ssm_w, sb_w)
        mats = _s5_mats(ssm_a_re[l], ssm_a_im[l], ssm_log_dt[l], ssm_b_re[l], ssm_b_im[l],
                        ssm_c_re[l], ssm_c_im[l])
        y_s5 = _s5(u, mats, ssm_d[l], bsz)
        z_bound = (SB_HEAD_DIM * SB_Q_SCALE * SB_BOUND_SLACK
                   * jnp.max(jnp.abs(sb_g_q[l].astype(F32))) * jnp.max(jnp.abs(sb_g_k[l].astype(F32))))
        y_sb = _sb_attn(q.reshape(bsz, seq, sb_w), k.reshape(bsz, seq, sb_w), v.reshape(bsz, seq, sb_w), z_bound)
        x1 = _mix_out(x.reshape(n, d), y_s5, y_sb.reshape(n, sb_w), ssm_w_glu[l],
                      g_out_ssm[l], g_out_sb[l], w_out[l])
        mk, mv = _mem_kv(mem.astype(F32), g_mem[l], xa_w_kv[l], xa_g_k[l])
        x2 = _xattn(x1.reshape(bsz, seq, d), g_xa[l], xa_w_q[l], xa_g_q[l], mk, mv, xa_w_o[l])
        x = _mlp(x2.reshape(n, d), g_mlp[l], w_up[l], w_down[l]).reshape(bsz, seq, d)
    return x
```

```python
import functools

import jax
import jax.numpy as jnp
from jax import lax
from jax.experimental import pallas as pl
from jax.experimental.pallas import tpu as pltpu

F32 = jnp.float32
BF16 = jnp.bfloat16

NORM_EPS = 1e-6
SSM_GROUP = 16
SSM_STATE = 64
SSM_CHUNK = 16
SB_HEAD_DIM = 64
XA_HEADS = 4
XA_HEAD_DIM = 128
LANES = 128
LOG2E = 1.4426950408889634
SB_Q_SCALE = SB_HEAD_DIM ** -0.5 * LOG2E
SB_BOUND_SLACK = 1.05
VMEM_LIMIT_BYTES = 56 * 1024 * 1024


def _params(*sem):
    return pltpu.CompilerParams(dimension_semantics=sem, vmem_limit_bytes=VMEM_LIMIT_BYTES)


def _const_spec(shape):
    zeros = (0,) * len(shape)
    return pl.BlockSpec(shape, lambda *_: zeros, pipeline_mode=pl.Buffered(1))


def _rms(x, g):
    return x * lax.rsqrt(jnp.mean(x * x, axis=-1, keepdims=True) + NORM_EPS) * g


def _token_tile(n, want):
    t = min(n, want)
    assert n % t == 0
    return t


def _in_proj_kernel(x_ref, g_ref, w_ref, seg_ref, gq_ref, gk_ref,
                    u_ref, q_ref, k_ref, v_ref, *, ssm_w, sb_w):
    h = _rms(x_ref[...], g_ref[...])
    proj = jnp.dot(h.astype(BF16), w_ref[...], preferred_element_type=F32)
    u_ref[...] = proj[:, :ssm_w]
    q = proj[:, ssm_w:ssm_w + sb_w]
    k = proj[:, ssm_w + sb_w:ssm_w + 2 * sb_w]
    v_ref[...] = proj[:, ssm_w + 2 * sb_w:].astype(BF16)
    seg = seg_ref[...]

    def head_norm(t, g):
        ms = jnp.dot((t * t).astype(BF16), seg, preferred_element_type=F32)
        return t * lax.rsqrt(ms + NORM_EPS) * g

    q_ref[...] = head_norm(q, gq_ref[...]).astype(BF16)
    k_ref[...] = head_norm(k, gk_ref[...]).astype(BF16)


def _in_proj(x2, g_mix, w_in, g_q, g_k, ssm_w, sb_w):
    n, d = x2.shape
    tm = _token_tile(n, 512)
    heads = sb_w // SB_HEAD_DIM
    lane_head = jnp.arange(sb_w) // SB_HEAD_DIM
    seg = (lane_head[:, None] == lane_head[None, :]).astype(F32) / SB_HEAD_DIM
    gq = (jnp.tile(g_q.astype(F32), heads) * SB_Q_SCALE)[None]
    gk = jnp.tile(g_k.astype(F32), heads)[None]
    kern = functools.partial(_in_proj_kernel, ssm_w=ssm_w, sb_w=sb_w)
    tok = lambda w: pl.BlockSpec((tm, w), lambda i: (i, 0))
    return pl.pallas_call(
        kern,
        grid=(n // tm,),
        in_specs=[tok(d), _const_spec((1, d)), _const_spec(w_in.shape), _const_spec((sb_w, sb_w)),
                  _const_spec((1, sb_w)), _const_spec((1, sb_w))],
        out_specs=[tok(ssm_w), tok(sb_w), tok(sb_w), tok(sb_w)],
        out_shape=[jax.ShapeDtypeStruct((n, ssm_w), F32)] + [jax.ShapeDtypeStruct((n, sb_w), BF16)] * 3,
        compiler_params=_params("parallel"),
        name="in_proj",
    )(x2, g_mix.astype(F32)[None], w_in.astype(BF16), seg.astype(BF16), gq, gk)


def _s5_mats(a_re, a_im, log_dt, b_re, b_im, c_re, c_im):
    hp = lax.Precision.HIGHEST
    L = SSM_CHUNK
    g, p = a_re.shape
    nc = b_re.shape[-1]
    lr, li = a_re.astype(F32), a_im.astype(F32)
    dt = jnp.exp(log_dt.astype(F32))[:, None]
    n = jnp.arange(L + 1, dtype=F32)[:, None, None]
    mag = jnp.exp(n * (lr * dt)[None])
    pr, pi = mag * jnp.cos(n * (li * dt)[None]), mag * jnp.sin(n * (li * dt)[None])
    xr, xi = pr[1] - 1.0, pi[1]
    den = lr * lr + li * li
    fr, fi = (xr * lr + xi * li) / den, (xi * lr - xr * li) / den
    br, bi = b_re.astype(F32), b_im.astype(F32)
    bbr = fr[..., None] * br - fi[..., None] * bi
    bbi = fr[..., None] * bi + fi[..., None] * br
    cr, ci = c_re.astype(F32), c_im.astype(F32)
    mr = pr[..., None] * bbr[None] - pi[..., None] * bbi[None]
    mi = pr[..., None] * bbi[None] + pi[..., None] * bbr[None]
    kern = (jnp.einsum('gcp,ngpd->ngcd', cr, mr[:L], precision=hp)
            - jnp.einsum('gcp,ngpd->ngcd', ci, mi[:L], precision=hp))
    rev = L - 1 - jnp.arange(L)
    pcr = mr[rev].transpose(1, 0, 3, 2).reshape(g, L * nc, p)
    pci = mi[rev].transpose(1, 0, 3, 2).reshape(g, L * nc, p)
    p_mat = jnp.concatenate([pcr, pci, pci, pcr], axis=-1)
    qr = cr[None] * pr[1:, :, None, :] - ci[None] * pi[1:, :, None, :]
    qi = cr[None] * pi[1:, :, None, :] + ci[None] * pr[1:, :, None, :]
    qr = qr.transpose(1, 3, 0, 2).reshape(g, p, L * nc)
    qi = qi.transpose(1, 3, 0, 2).reshape(g, p, L * nc)
    q_mat = jnp.concatenate([qr, -qi], axis=1)
    coef = jnp.stack([jnp.concatenate([pr[L], pr[L]], -1),
                      jnp.concatenate([-pi[L], pi[L]], -1)], axis=1)
    gt = LANES // nc
    nt = g // gt
    eye = jnp.eye(gt, dtype=F32)
    k5 = kern.reshape(L, nt, gt, nc, nc).transpose(1, 0, 2, 4, 3)
    bdk = k5[:, :, :, :, None, :] * eye[None, None, :, None, :, None]
    bdk = bdk.reshape(nt, L * LANES, LANES)
    pc = p_mat.reshape(nt, gt, L, nc, 4 * p).transpose(0, 2, 1, 3, 4).reshape(nt, L * LANES, 4 * p)
    qc = q_mat.reshape(nt, gt * 2 * p, L * nc)
    cf = coef.reshape(nt, gt, 2, 2 * p).transpose(0, 2, 1, 3).reshape(nt, 2, 2 * p * gt)
    col = jnp.arange(L * LANES)
    spread = (jnp.arange(L * nc)[:, None] == (col // LANES * nc + col % nc)[None, :])
    return bdk.astype(BF16), pc.astype(BF16), qc.astype(BF16), spread.astype(BF16), cf.astype(F32)


def _s5_kernel(u_ref, bdk_ref, pc_ref, qc_ref, spread_ref, coef_ref, d_ref, y_ref,
               tt_scr, pp_scr, qq_scr, e_scr, s_scr, *, sw, nc):
    L = SSM_CHUNK
    gt = LANES // nc

    @pl.when(pl.program_id(1) == 0)
    def _expand_tile_matrices():
        zero = jnp.zeros((LANES, LANES), BF16)
        for j in range(L):
            for i in range(L):
                lag = (i - j) * LANES
                tt_scr[j * LANES:(j + 1) * LANES, i * LANES:(i + 1) * LANES] = (
                    bdk_ref[0, lag:lag + LANES, :] if i >= j else zero)
        pc = pc_ref[0].astype(F32)
        row_g = lax.broadcasted_iota(jnp.int32, (L * LANES, LANES), 0) // nc % gt
        for half in range(2):
            src = pc[:, half * LANES:(half + 1) * LANES]
            for g in range(gt):
                lo = half * sw + g * LANES
                pp_scr[:, lo:lo + LANES] = jnp.where(row_g == g, src, 0.0).astype(BF16)
        full = jnp.dot(qc_ref[0], spread_ref[...], preferred_element_type=F32)
        rg = lax.broadcasted_iota(jnp.int32, full.shape, 0) // LANES
        cg = lax.broadcasted_iota(jnp.int32, full.shape, 1) // nc % gt
        qq_scr[...] = jnp.where(rg == cg, full, 0.0).astype(BF16)

    nk = u_ref.shape[0]
    u_steps = jnp.swapaxes(u_ref[...], 0, 1)
    u = jnp.concatenate([u_steps[t] for t in range(L)], axis=1).astype(BF16)
    e_scr[...] = jnp.dot(u, pp_scr[...], preferred_element_type=F32)
    c1 = coef_ref[0, 0:1, :]
    c2 = coef_ref[0, 1:2, :]

    def step(k, carry):
        s, w = carry
        s_scr[pl.ds(k, 1), :] = s
        e = e_scr[pl.ds(k, 1), :]
        return s * c1 + w * c2 + e[:, :sw], w * c1 - s * c2 + e[:, sw:]

    zero = jnp.zeros((1, sw), F32)
    lax.fori_loop(0, nk, step, (zero, zero), unroll=8)
    y = jnp.dot(u, tt_scr[...], preferred_element_type=F32)
    y += jnp.dot(s_scr[...].astype(BF16), qq_scr[...], preferred_element_type=F32)
    y_steps = jnp.stack([y[:, i * LANES:(i + 1) * LANES] for i in range(L)], axis=0)
    y_ref[...] = jnp.swapaxes(y_steps, 0, 1) + d_ref[...] * u_ref[...]


def _s5(u, mats, d_skip, bsz):
    bdk, pc, qc, spread, cf = mats
    n, width = u.shape
    L = SSM_CHUNK
    nt = width // LANES
    nk = n // L // bsz
    lw, sw = L * LANES, cf.shape[-1]
    nc = qc.shape[-1] // L
    kern = functools.partial(_s5_kernel, sw=sw, nc=nc)
    per_t = lambda a, b: pl.BlockSpec((1, a, b), lambda t, s: (t, 0, 0))
    seq_blk = pl.BlockSpec((nk, L, LANES), lambda t, s: (s, 0, t))
    y = pl.pallas_call(
        kern,
        grid=(nt, bsz),
        in_specs=[seq_blk, per_t(lw, LANES), per_t(lw, 2 * LANES), per_t(sw, L * nc),
                  _const_spec(spread.shape), per_t(2, sw), per_t(1, LANES)],
        out_specs=seq_blk,
        out_shape=jax.ShapeDtypeStruct((n // L, L, width), F32),
        scratch_shapes=[pltpu.VMEM((lw, lw), BF16), pltpu.VMEM((lw, 2 * sw), BF16), pltpu.VMEM((sw, lw), BF16),
                        pltpu.VMEM((nk, 2 * sw), F32), pltpu.VMEM((nk, sw), F32)],
        compiler_params=_params("parallel", "arbitrary"),
        name="s5_scan",
    )(u.reshape(n // L, L, width), bdk, pc, qc, spread, cf, d_skip.astype(F32).reshape(nt, 1, LANES))
    return y.reshape(n, width)


F32_ZERO_EXP2 = 160.0


def _sb_attn_kernel(thr_ref, q_ref, k_ref, v_ref, tri_ref, o_ref, *, blk, nlb):
    i = pl.program_id(1)
    lane = lax.broadcasted_iota(jnp.int32, (blk, LANES), 1)
    first = lane < SB_HEAD_DIM
    qs = []
    for c in range(nlb):
        q = q_ref[0, :, c * LANES:(c + 1) * LANES]
        zq = jnp.zeros_like(q)
        qs.append(jnp.concatenate([jnp.where(first, q, zq), jnp.where(first, zq, q)], axis=0))
    tri = tri_ref[...]
    row = lax.broadcasted_iota(jnp.int32, (2 * blk, blk), 0)
    col = lax.broadcasted_iota(jnp.int32, (2 * blk, blk), 1)
    below = col < jnp.where(row >= blk, row - blk, row)

    def block(j, carry, acc, mask):
        start = pl.multiple_of(j * blk, blk)
        zs, ps = [], []
        for c in range(nlb):
            kb = k_ref[0, pl.ds(start, blk), c * LANES:(c + 1) * LANES]
            zs.append(lax.dot_general(qs[c], kb, (((1,), (1,)), ((), ())), preferred_element_type=F32))
        for c in range(nlb):
            z = zs[c]
            sp = jnp.maximum(z, 0.0) + jnp.log(1.0 + jnp.exp2(-jnp.abs(z))) * LOG2E
            if mask is not None:
                sp = jnp.where(mask, sp, 0.0)
                zs[c] = jnp.where(mask, z, -1e30)
            hi = sp.astype(BF16)
            lo = (sp - hi.astype(F32)).astype(BF16)
            ps.append(jnp.concatenate([hi, lo], axis=1))
        cums = [jnp.dot(ps[c], tri, preferred_element_type=F32) for c in range(nlb)]
        ws = [jnp.exp2(zs[c] - cums[c] - carry[c]).astype(BF16) for c in range(nlb)]
        new_acc, new_carry = [], []
        for c in range(nlb):
            vb = v_ref[0, pl.ds(start, blk), c * LANES:(c + 1) * LANES]
            new_acc.append(acc[c] + jnp.dot(ws[c], vb, preferred_element_type=F32))
            new_carry.append(carry[c] + cums[c][:, 0:1])
        return tuple(new_carry), tuple(new_acc)

    zc = tuple(jnp.zeros((2 * blk, 1), F32) for _ in range(nlb))
    za = tuple(jnp.zeros((2 * blk, LANES), F32) for _ in range(nlb))
    carry, acc = block(i, zc, za, below)
    thr = thr_ref[0]

    def more(st):
        return jnp.logical_and(st[0] >= 0, st[1] <= thr)

    def step(st):
        j, _, carry, acc = st
        carry, acc = block(j, carry, acc, None)
        return j - 1, jnp.min(functools.reduce(jnp.minimum, carry)), carry, acc

    _, _, carry, acc = lax.while_loop(more, step, (i - 1, jnp.float32(0.0), carry, acc))
    for c in range(nlb):
        o_ref[0, :, c * LANES:(c + 1) * LANES] = jnp.where(first, acc[c][:blk], acc[c][blk:]).astype(o_ref.dtype)


def _sb_attn(q, k, v, z_bound):
    bsz, seq, width = q.shape
    blk = _token_tile(seq, 256)
    assert width % LANES == 0
    tri = (jnp.arange(blk)[:, None] >= jnp.arange(blk)[None, :]).astype(BF16)
    tri = jnp.concatenate([tri, tri], axis=0)
    thr = (z_bound + F32_ZERO_EXP2).astype(F32).reshape(1)
    kern = functools.partial(_sb_attn_kernel, blk=blk, nlb=width // LANES)
    q_blk = pl.BlockSpec((1, blk, width), lambda b, i: (b, i, 0))
    whole_seq = pl.BlockSpec((1, seq, width), lambda b, i: (b, 0, 0), pipeline_mode=pl.Buffered(1))
    return pl.pallas_call(
        kern,
        grid=(bsz, seq // blk),
        in_specs=[pl.BlockSpec(memory_space=pltpu.SMEM), q_blk, whole_seq, whole_seq,
                  _const_spec((2 * blk, blk))],
        out_specs=q_blk,
        out_shape=jax.ShapeDtypeStruct((bsz, seq, width), BF16),
        compiler_params=_params("parallel", "arbitrary"),
        name="sb_attn",
    )(thr, q, k, v, tri)


def _mix_out_kernel(x_ref, ys_ref, sb_ref, wg_ref, gs_ref, gb_ref, wo_ref, o_ref, *, ssm_w):
    y = jax.nn.gelu(ys_ref[...])
    gate = jnp.dot(y.astype(BF16), wg_ref[...], preferred_element_type=F32)
    y = y * jax.nn.sigmoid(gate)
    yn = _rms(y, gs_ref[...]).astype(BF16)
    sn = _rms(sb_ref[...].astype(F32), gb_ref[...]).astype(BF16)
    out = jnp.dot(yn, wo_ref[:ssm_w, :], preferred_element_type=F32)
    out += jnp.dot(sn, wo_ref[ssm_w:, :], preferred_element_type=F32)
    o_ref[...] = x_ref[...] + out


def _mix_out(x2, y_s5, y_sb, w_glu, g_ssm, g_sb, w_out):
    n, d = x2.shape
    ssm_w, sb_w = y_s5.shape[1], y_sb.shape[1]
    tm = _token_tile(n, 512)
    kern = functools.partial(_mix_out_kernel, ssm_w=ssm_w)
    tok = lambda w: pl.BlockSpec((tm, w), lambda i: (i, 0))
    return pl.pallas_call(
        kern,
        grid=(n // tm,),
        in_specs=[tok(d), tok(ssm_w), tok(sb_w), _const_spec(w_glu.shape), _const_spec((1, ssm_w)),
                  _const_spec((1, sb_w)), _const_spec(w_out.shape)],
        out_specs=tok(d),
        out_shape=jax.ShapeDtypeStruct((n, d), F32),
        compiler_params=_params("parallel"),
        name="mix_out",
    )(x2, y_s5, y_sb, w_glu.astype(BF16), g_ssm.astype(F32)[None], g_sb.astype(F32)[None], w_out.astype(BF16))


def _mem_kv_kernel(m_ref, g_ref, w_ref, gk_ref, k_ref, v_ref, *, xa_w):
    h = _rms(m_ref[0], g_ref[...])
    kv = jnp.dot(h.astype(BF16), w_ref[...], preferred_element_type=F32)
    for hd in range(xa_w // XA_HEAD_DIM):
        sl = slice(hd * XA_HEAD_DIM, (hd + 1) * XA_HEAD_DIM)
        k_ref[0, :, sl] = _rms(kv[:, sl], gk_ref[...]).astype(BF16)
    v_ref[0] = kv[:, xa_w:].astype(BF16)


def _mem_kv(mem, g_mem, w_kv, g_k):
    bsz, mlen, d = mem.shape
    xa_w = w_kv.shape[1] // 2
    kern = functools.partial(_mem_kv_kernel, xa_w=xa_w)
    per_b = lambda w: pl.BlockSpec((1, mlen, w), lambda b: (b, 0, 0))
    return pl.pallas_call(
        kern,
        grid=(bsz,),
        in_specs=[per_b(d), _const_spec((1, d)), _const_spec(w_kv.shape), _const_spec((1, XA_HEAD_DIM))],
        out_specs=[per_b(xa_w), per_b(xa_w)],
        out_shape=[jax.ShapeDtypeStruct((bsz, mlen, xa_w), BF16)] * 2,
        compiler_params=_params("parallel"),
        name="mem_kv",
    )(mem, g_mem.astype(F32)[None], w_kv.astype(BF16), g_k.astype(F32)[None])


def _xattn_kernel(x_ref, g_ref, wq_ref, gq_ref, k_ref, v_ref, wo_ref, o_ref):
    x = x_ref[0]
    h = _rms(x, g_ref[...])
    q = jnp.dot(h.astype(BF16), wq_ref[...], preferred_element_type=F32)
    heads = []
    for hd in range(q.shape[1] // XA_HEAD_DIM):
        sl = slice(hd * XA_HEAD_DIM, (hd + 1) * XA_HEAD_DIM)
        qh = _rms(q[:, sl], gq_ref[...]).astype(BF16)
        s = lax.dot_general(qh, k_ref[0, :, sl], (((1,), (1,)), ((), ())), preferred_element_type=F32)
        e = jnp.exp(s - jnp.max(s, axis=-1, keepdims=True))
        pv = jnp.dot(e.astype(BF16), v_ref[0, :, sl], preferred_element_type=F32)
        heads.append((pv / jnp.sum(e, axis=-1, keepdims=True)).astype(BF16))
    o = jnp.concatenate(heads, axis=-1)
    o_ref[0] = x + jnp.dot(o, wo_ref[...], preferred_element_type=F32)


def _xattn(x, g_xa, w_q, g_q, k, v, w_o):
    bsz, seq, d = x.shape
    mlen, xa_w = k.shape[1], k.shape[2]
    tm = _token_tile(seq, 512)
    gq = (g_q.astype(F32) * (XA_HEAD_DIM ** -0.5))[None]
    return pl.pallas_call(
        _xattn_kernel,
        grid=(bsz, seq // tm),
        in_specs=[pl.BlockSpec((1, tm, d), lambda b, i: (b, i, 0)), _const_spec((1, d)),
                  _const_spec(w_q.shape), _const_spec((1, XA_HEAD_DIM)),
                  pl.BlockSpec((1, mlen, xa_w), lambda b, i: (b, 0, 0)),
                  pl.BlockSpec((1, mlen, xa_w), lambda b, i: (b, 0, 0)),
                  _const_spec(w_o.shape)],
        out_specs=pl.BlockSpec((1, tm, d), lambda b, i: (b, i, 0)),
        out_shape=jax.ShapeDtypeStruct((bsz, seq, d), F32),
        compiler_params=_params("parallel", "parallel"),
        name="xattn",
    )(x, g_xa.astype(F32)[None], w_q.astype(BF16), gq, k, v, w_o.astype(BF16))


def _mlp_kernel(x_ref, g_ref, wu_ref, wd_ref, o_ref, *, ff_chunk):
    x = x_ref[...]
    h = _rms(x, g_ref[...]).astype(BF16)
    acc = x
    for c in range(wu_ref.shape[1] // ff_chunk):
        sl = slice(c * ff_chunk, (c + 1) * ff_chunk)
        a = jnp.maximum(jnp.dot(h, wu_ref[:, sl], preferred_element_type=F32), 0.0)
        acc = acc + jnp.dot((a * a).astype(BF16), wd_ref[sl, :], preferred_element_type=F32)
    o_ref[...] = acc


def _mlp(x2, g_mlp, w_up, w_down):
    n, d = x2.shape
    tm = _token_tile(n, 512)
    ff_chunk = min(w_up.shape[1], 1024)
    kern = functools.partial(_mlp_kernel, ff_chunk=ff_chunk)
    tok = pl.BlockSpec((tm, d), lambda i: (i, 0))
    return pl.pallas_call(
        kern,
        grid=(n // tm,),
        in_specs=[tok, _const_spec((1, d)), _const_spec(w_up.shape), _const_spec(w_down.shape)],
        out_specs=tok,
        out_shape=jax.ShapeDtypeStruct((n, d), F32),
        compiler_params=_params("parallel"),
        name="mlp",
    )(x2, g_mlp.astype(F32)[None], w_up.astype(BF16), w_down.astype(BF16))


def kernel(x, mem, g_mix, w_in, ssm_a_re, ssm_a_im, ssm_log_dt, ssm_b_re, ssm_b_im, ssm_c_re, ssm_c_im,
           ssm_d, ssm_w_glu, sb_g_q, sb_g_k, g_out_ssm, g_out_sb, w_out, g_xa, g_mem, xa_w_q, xa_w_kv,
           xa_g_q, xa_g_k, xa_w_o, g_mlp, w_up, w_down):
    bsz, seq, d = x.shape
    n = bsz * seq
    ssm_w = ssm_d.shape[-1]
    sb_w = g_out_sb.shape[-1]
    x = x.astype(F32)
    for l in range(g_mix.shape[0]):
        u, q, k, v = _in_proj(x.reshape(n, d), g_mix[l], w_in[l], sb_g_q[l], sb_g_k[l], ssm_w, sb_w)
        mats = _s5_mats(ssm_a_re[l], ssm_a_im[l], ssm_log_dt[l], ssm_b_re[l], ssm_b_im[l],
                        ssm_c_re[l], ssm_c_im[l])
        y_s5 = _s5(u, mats, ssm_d[l], bsz)
        z_bound = (SB_HEAD_DIM * SB_Q_SCALE * SB_BOUND_SLACK
                   * jnp.max(jnp.abs(sb_g_q[l].astype(F32))) * jnp.max(jnp.abs(sb_g_k[l].astype(F32))))
        y_sb = _sb_attn(q.reshape(bsz, seq, sb_w), k.reshape(bsz, seq, sb_w), v.reshape(bsz, seq, sb_w), z_bound)
        x1 = _mix_out(x.reshape(n, d), y_s5, y_sb.reshape(n, sb_w), ssm_w_glu[l],
                      g_out_ssm[l], g_out_sb[l], w_out[l])
        mk, mv = _mem_kv(mem.astype(F32), g_mem[l], xa_w_kv[l], xa_g_k[l])
        x2 = _xattn(x1.reshape(bsz, seq, d), g_xa[l], xa_w_q[l], xa_g_q[l], mk, mv, xa_w_o[l])
        x = _mlp(x2.reshape(n, d), g_mlp[l], w_up[l], w_down[l]).reshape(bsz, seq, d)
    return x
```

```python
import functools

import jax
import jax.numpy as jnp
from jax import lax
from jax.experimental import pallas as pl
from jax.experimental.pallas import tpu as pltpu

F32 = jnp.float32
BF16 = jnp.bfloat16

NORM_EPS = 1e-6
SSM_GROUP = 16
SSM_STATE = 64
SSM_CHUNK = 16
S5_COL_GROUPS = 4
SB_HEAD_DIM = 64
XA_HEADS = 4
XA_HEAD_DIM = 128
LANES = 128
LOG2E = 1.4426950408889634
SB_Q_SCALE = SB_HEAD_DIM ** -0.5 * LOG2E
SB_BOUND_SLACK = 1.05
VMEM_LIMIT_BYTES = 56 * 1024 * 1024


def _params(*sem):
    return pltpu.CompilerParams(dimension_semantics=sem, vmem_limit_bytes=VMEM_LIMIT_BYTES)


def _const_spec(shape):
    zeros = (0,) * len(shape)
    return pl.BlockSpec(shape, lambda *_: zeros, pipeline_mode=pl.Buffered(1))


def _rms(x, g):
    return x * lax.rsqrt(jnp.mean(x * x, axis=-1, keepdims=True) + NORM_EPS) * g


def _token_tile(n, want):
    t = min(n, want)
    assert n % t == 0
    return t


def _in_proj_kernel(x_ref, g_ref, w_ref, seg_ref, gq_ref, gk_ref,
                    u_ref, q_ref, k_ref, v_ref, *, ssm_w, sb_w):
    h = _rms(x_ref[...], g_ref[...])
    proj = jnp.dot(h.astype(BF16), w_ref[...], preferred_element_type=F32)
    u_ref[...] = proj[:, :ssm_w]
    q = proj[:, ssm_w:ssm_w + sb_w]
    k = proj[:, ssm_w + sb_w:ssm_w + 2 * sb_w]
    v_ref[...] = proj[:, ssm_w + 2 * sb_w:].astype(BF16)
    seg = seg_ref[...]

    def head_norm(t, g):
        ms = jnp.dot((t * t).astype(BF16), seg, preferred_element_type=F32)
        return t * lax.rsqrt(ms + NORM_EPS) * g

    q_ref[...] = head_norm(q, gq_ref[...]).astype(BF16)
    k_ref[...] = head_norm(k, gk_ref[...]).astype(BF16)


def _in_proj(x2, g_mix, w_in, g_q, g_k, ssm_w, sb_w):
    n, d = x2.shape
    tm = _token_tile(n, 512)
    heads = sb_w // SB_HEAD_DIM
    lane_head = jnp.arange(sb_w) // SB_HEAD_DIM
    seg = (lane_head[:, None] == lane_head[None, :]).astype(F32) / SB_HEAD_DIM
    gq = (jnp.tile(g_q.astype(F32), heads) * SB_Q_SCALE)[None]
    gk = jnp.tile(g_k.astype(F32), heads)[None]
    kern = functools.partial(_in_proj_kernel, ssm_w=ssm_w, sb_w=sb_w)
    tok = lambda w: pl.BlockSpec((tm, w), lambda i: (i, 0))
    return pl.pallas_call(
        kern,
        grid=(n // tm,),
        in_specs=[tok(d), _const_spec((1, d)), _const_spec(w_in.shape), _const_spec((sb_w, sb_w)),
                  _const_spec((1, sb_w)), _const_spec((1, sb_w))],
        out_specs=[tok(ssm_w), tok(sb_w), tok(sb_w), tok(sb_w)],
        out_shape=[jax.ShapeDtypeStruct((n, ssm_w), F32)] + [jax.ShapeDtypeStruct((n, sb_w), BF16)] * 3,
        compiler_params=_params("parallel"),
        name="in_proj",
    )(x2, g_mix.astype(F32)[None], w_in.astype(BF16), seg.astype(BF16), gq, gk)


def _s5_mats(a_re, a_im, log_dt, b_re, b_im, c_re, c_im):
    hp = lax.Precision.HIGHEST
    L = SSM_CHUNK
    g, p = a_re.shape
    nc = b_re.shape[-1]
    lr, li = a_re.astype(F32), a_im.astype(F32)
    dt = jnp.exp(log_dt.astype(F32))[:, None]
    n = jnp.arange(L + 1, dtype=F32)[:, None, None]
    mag = jnp.exp(n * (lr * dt)[None])
    pr, pi = mag * jnp.cos(n * (li * dt)[None]), mag * jnp.sin(n * (li * dt)[None])
    xr, xi = pr[1] - 1.0, pi[1]
    den = lr * lr + li * li
    fr, fi = (xr * lr + xi * li) / den, (xi * lr - xr * li) / den
    br, bi = b_re.astype(F32), b_im.astype(F32)
    bbr = fr[..., None] * br - fi[..., None] * bi
    bbi = fr[..., None] * bi + fi[..., None] * br
    cr, ci = c_re.astype(F32), c_im.astype(F32)
    mr = pr[..., None] * bbr[None] - pi[..., None] * bbi[None]
    mi = pr[..., None] * bbi[None] + pi[..., None] * bbr[None]
    kern = (jnp.einsum('gcp,ngpd->ngcd', cr, mr[:L], precision=hp)
            - jnp.einsum('gcp,ngpd->ngcd', ci, mi[:L], precision=hp))
    rev = L - 1 - jnp.arange(L)
    pcr = mr[rev].transpose(1, 0, 3, 2).reshape(g, L * nc, p)
    pci = mi[rev].transpose(1, 0, 3, 2).reshape(g, L * nc, p)
    p_mat = jnp.concatenate([pcr, pci], axis=-1)
    qr = cr[None] * pr[1:, :, None, :] - ci[None] * pi[1:, :, None, :]
    qi = cr[None] * pi[1:, :, None, :] + ci[None] * pr[1:, :, None, :]
    qr = qr.transpose(1, 3, 0, 2).reshape(g, p, L * nc)
    qi = qi.transpose(1, 3, 0, 2).reshape(g, p, L * nc)
    q_mat = jnp.concatenate([qr, -qi], axis=1)
    coef = jnp.stack([jnp.concatenate([pr[L], pr[L]], -1),
                      jnp.concatenate([-pi[L], pi[L]], -1)], axis=1)
    gt = LANES // nc
    nt = g // gt
    eye = jnp.eye(gt, dtype=F32)
    k5 = kern.reshape(L, nt, gt, nc, nc).transpose(1, 0, 2, 4, 3)
    bdk = k5[:, :, :, :, None, :] * eye[None, None, :, None, :, None]
    bdk = bdk.reshape(nt, L * LANES, LANES)
    pc = p_mat.reshape(nt, gt, L, nc, 2 * p).transpose(0, 2, 1, 3, 4).reshape(nt, L * LANES, 2 * p)
    qc = q_mat.reshape(nt, gt * 2 * p, L * nc)
    cf = coef.reshape(nt, gt, 2, 2 * p).transpose(0, 2, 1, 3).reshape(nt, 2, 2 * p * gt)
    col = jnp.arange(L * LANES)
    spread = (jnp.arange(L * nc)[:, None] == (col // LANES * nc + col % nc)[None, :])
    return bdk.astype(BF16), pc.astype(BF16), qc.astype(BF16), spread.astype(BF16), cf.astype(F32)


def _s5_kernel(u_ref, bdk_ref, pc_ref, qc_ref, spread_ref, coef_ref, d_ref, y_ref,
               tt_scr, pp_scr, qq_scr, e_scr, s_scr, *, sw, nc):
    L = SSM_CHUNK
    gt = LANES // nc
    lw = L * LANES

    @pl.when(pl.program_id(1) == 0)
    def _expand_tile_matrices():
        zero = jnp.zeros((LANES, LANES), BF16)
        for j in range(L):
            for i in range(L):
                lag = (i - j) * LANES
                tt_scr[j * LANES:(j + 1) * LANES, i * LANES:(i + 1) * LANES] = (
                    bdk_ref[0, lag:lag + LANES, :] if i >= j else zero)
        pc = pc_ref[0].astype(F32)
        row_g = lax.broadcasted_iota(jnp.int32, (L * LANES, LANES), 0) // nc % gt
        for g in range(gt):
            pp_scr[:, g * LANES:(g + 1) * LANES] = jnp.where(row_g == g, pc, 0.0).astype(BF16)
        full = jnp.dot(qc_ref[0], spread_ref[...], preferred_element_type=F32)
        rg = lax.broadcasted_iota(jnp.int32, full.shape, 0) // LANES
        cg = lax.broadcasted_iota(jnp.int32, full.shape, 1) // nc % gt
        qq_scr[...] = jnp.where(rg == cg, full, 0.0).astype(BF16)

    nk = u_ref.shape[0]
    u_steps = jnp.swapaxes(u_ref[...], 0, 1)
    u = jnp.concatenate([u_steps[t] for t in range(L)], axis=1).astype(BF16)
    e_s = jnp.dot(u, pp_scr[...], preferred_element_type=F32)
    e_scr[:, :sw] = e_s
    for g in range(gt):
        e_scr[:, sw + g * LANES:sw + (g + 1) * LANES] = pltpu.roll(
            e_s[:, g * LANES:(g + 1) * LANES], LANES // 2, axis=1)
    c1 = coef_ref[0, 0:1, :]
    c2 = coef_ref[0, 1:2, :]

    def step(k, carry):
        s, w = carry
        s_scr[pl.ds(k, 1), :] = s
        e = e_scr[pl.ds(k, 1), :]
        return s * c1 + w * c2 + e[:, :sw], w * c1 - s * c2 + e[:, sw:]

    zero = jnp.zeros((1, sw), F32)
    lax.fori_loop(0, nk, step, (zero, zero), unroll=8)
    cw = lw // S5_COL_GROUPS
    y = jnp.concatenate(
        [jnp.dot(u[:, :(n + 1) * cw], tt_scr[:(n + 1) * cw, n * cw:(n + 1) * cw], preferred_element_type=F32)
         for n in range(S5_COL_GROUPS)], axis=1)
    y += jnp.dot(s_scr[...].astype(BF16), qq_scr[...], preferred_element_type=F32)
    y_steps = jnp.stack([y[:, i * LANES:(i + 1) * LANES] for i in range(L)], axis=0)
    y_ref[...] = jnp.swapaxes(y_steps, 0, 1) + d_ref[...] * u_ref[...]


def _s5(u, mats, d_skip, bsz):
    bdk, pc, qc, spread, cf = mats
    n, width = u.shape
    L = SSM_CHUNK
    nt = width // LANES
    nk = n // L // bsz
    lw, sw = L * LANES, cf.shape[-1]
    nc = qc.shape[-1] // L
    kern = functools.partial(_s5_kernel, sw=sw, nc=nc)
    per_t = lambda a, b: pl.BlockSpec((1, a, b), lambda t, s: (t, 0, 0))
    seq_blk = pl.BlockSpec((nk, L, LANES), lambda t, s: (s, 0, t))
    y = pl.pallas_call(
        kern,
        grid=(nt, bsz),
        in_specs=[seq_blk, per_t(lw, LANES), per_t(lw, LANES), per_t(sw, L * nc),
                  _const_spec(spread.shape), per_t(2, sw), per_t(1, LANES)],
        out_specs=seq_blk,
        out_shape=jax.ShapeDtypeStruct((n // L, L, width), F32),
        scratch_shapes=[pltpu.VMEM((lw, lw), BF16), pltpu.VMEM((lw, sw), BF16), pltpu.VMEM((sw, lw), BF16),
                        pltpu.VMEM((nk, 2 * sw), F32), pltpu.VMEM((nk, sw), F32)],
        compiler_params=_params("parallel", "arbitrary"),
        name="s5_scan",
    )(u.reshape(n // L, L, width), bdk, pc, qc, spread, cf, d_skip.astype(F32).reshape(nt, 1, LANES))
    return y.reshape(n, width)


F32_ZERO_EXP2 = 160.0
SB_WAVE = 2


def _sb_attn_kernel(thr_ref, q_ref, k_ref, v_ref, tri_ref, o_ref, *, blk, nlb):
    i = pl.program_id(1)
    lane = lax.broadcasted_iota(jnp.int32, (blk, LANES), 1)
    first = lane < SB_HEAD_DIM
    qs = []
    for c in range(nlb):
        q = q_ref[0, :, c * LANES:(c + 1) * LANES]
        zq = jnp.zeros_like(q)
        qs.append(jnp.concatenate([jnp.where(first, q, zq), jnp.where(first, zq, q)], axis=0))
    tri = tri_ref[...]
    row = lax.broadcasted_iota(jnp.int32, (2 * blk, blk), 0)
    col = lax.broadcasted_iota(jnp.int32, (2 * blk, blk), 1)
    below = col < jnp.where(row >= blk, row - blk, row)

    def block(j, carry, acc, mask):
        start = pl.multiple_of(j * blk, blk)
        new_acc, new_carry = list(acc), list(carry)
        for c0 in range(0, nlb, SB_WAVE):
            wave = range(c0, min(nlb, c0 + SB_WAVE))
            zs, ps = {}, {}
            for c in wave:
                kb = k_ref[0, pl.ds(start, blk), c * LANES:(c + 1) * LANES]
                zs[c] = lax.dot_general(qs[c], kb, (((1,), (1,)), ((), ())), preferred_element_type=F32)
            for c in wave:
                z = zs[c]
                sp = jnp.maximum(z, 0.0) + jnp.log(1.0 + jnp.exp2(-jnp.abs(z))) * LOG2E
                if mask is not None:
                    sp = jnp.where(mask, sp, 0.0)
                    zs[c] = jnp.where(mask, z, -1e30)
                ps[c] = sp.astype(BF16)
            cums = {c: jnp.dot(ps[c], tri, preferred_element_type=F32) for c in wave}
            ws = {c: jnp.exp2(zs[c] - cums[c] - carry[c]).astype(BF16) for c in wave}
            for c in wave:
                vb = v_ref[0, pl.ds(start, blk), c * LANES:(c + 1) * LANES]
                new_acc[c] = acc[c] + jnp.dot(ws[c], vb, preferred_element_type=F32)
                new_carry[c] = carry[c] + cums[c][:, 0:1]
        return tuple(new_carry), tuple(new_acc)

    zc = tuple(jnp.zeros((2 * blk, 1), F32) for _ in range(nlb))
    za = tuple(jnp.zeros((2 * blk, LANES), F32) for _ in range(nlb))
    carry, acc = block(i, zc, za, below)
    thr = thr_ref[0]

    def more(st):
        return jnp.logical_and(st[0] >= 0, st[1] <= thr)

    def step(st):
        j, _, carry, acc = st
        carry, acc = block(j, carry, acc, None)
        return j - 1, jnp.min(functools.reduce(jnp.minimum, carry)), carry, acc

    _, _, carry, acc = lax.while_loop(more, step, (i - 1, jnp.float32(0.0), carry, acc))
    for c in range(nlb):
        o_ref[0, :, c * LANES:(c + 1) * LANES] = jnp.where(first, acc[c][:blk], acc[c][blk:]).astype(o_ref.dtype)


def _sb_attn(q, k, v, z_bound):
    bsz, seq, width = q.shape
    blk = _token_tile(seq, 256)
    assert width % LANES == 0
    tri = (jnp.arange(blk)[:, None] >= jnp.arange(blk)[None, :]).astype(BF16)
    thr = (z_bound + F32_ZERO_EXP2).astype(F32).reshape(1)
    kern = functools.partial(_sb_attn_kernel, blk=blk, nlb=width // LANES)
    q_blk = pl.BlockSpec((1, blk, width), lambda b, i: (b, i, 0))
    whole_seq = pl.BlockSpec((1, seq, width), lambda b, i: (b, 0, 0), pipeline_mode=pl.Buffered(1))
    return pl.pallas_call(
        kern,
        grid=(bsz, seq // blk),
        in_specs=[pl.BlockSpec(memory_space=pltpu.SMEM), q_blk, whole_seq, whole_seq,
                  _const_spec((blk, blk))],
        out_specs=q_blk,
        out_shape=jax.ShapeDtypeStruct((bsz, seq, width), BF16),
        compiler_params=_params("parallel", "arbitrary"),
        name="sb_attn",
    )(thr, q, k, v, tri)


def _mix_out_body(x, ys, sb, wg_ref, gs_ref, gb_ref, wo_ref):
    ssm_w = ys.shape[1]
    y = jax.nn.gelu(ys)
    gate = jnp.dot(y.astype(BF16), wg_ref[...], preferred_element_type=F32)
    y = y * jax.nn.sigmoid(gate)
    yn = _rms(y, gs_ref[...]).astype(BF16)
    sn = _rms(sb.astype(F32), gb_ref[...]).astype(BF16)
    out = jnp.dot(yn, wo_ref[:ssm_w, :], preferred_element_type=F32)
    out += jnp.dot(sn, wo_ref[ssm_w:, :], preferred_element_type=F32)
    return x + out


def _mem_kv_kernel(m_ref, g_ref, w_ref, gk_ref, k_ref, v_ref, *, xa_w):
    h = _rms(m_ref[0], g_ref[...])
    kv = jnp.dot(h.astype(BF16), w_ref[...], preferred_element_type=F32)
    for hd in range(xa_w // XA_HEAD_DIM):
        sl = slice(hd * XA_HEAD_DIM, (hd + 1) * XA_HEAD_DIM)
        k_ref[0, :, sl] = _rms(kv[:, sl], gk_ref[...]).astype(BF16)
    v_ref[0] = kv[:, xa_w:].astype(BF16)


def _mem_kv(mem, g_mem, w_kv, g_k):
    bsz, mlen, d = mem.shape
    xa_w = w_kv.shape[1] // 2
    kern = functools.partial(_mem_kv_kernel, xa_w=xa_w)
    per_b = lambda w: pl.BlockSpec((1, mlen, w), lambda b: (b, 0, 0))
    return pl.pallas_call(
        kern,
        grid=(bsz,),
        in_specs=[per_b(d), _const_spec((1, d)), _const_spec(w_kv.shape), _const_spec((1, XA_HEAD_DIM))],
        out_specs=[per_b(xa_w), per_b(xa_w)],
        out_shape=[jax.ShapeDtypeStruct((bsz, mlen, xa_w), BF16)] * 2,
        compiler_params=_params("parallel"),
        name="mem_kv",
    )(mem, g_mem.astype(F32)[None], w_kv.astype(BF16), g_k.astype(F32)[None])


def _xattn_body(x, g_ref, wq_ref, gq_ref, k_ref, v_ref, wo_ref):
    h = _rms(x, g_ref[...])
    q = jnp.dot(h.astype(BF16), wq_ref[...], preferred_element_type=F32)
    heads = []
    for hd in range(q.shape[1] // XA_HEAD_DIM):
        sl = slice(hd * XA_HEAD_DIM, (hd + 1) * XA_HEAD_DIM)
        qh = _rms(q[:, sl], gq_ref[...]).astype(BF16)
        s = lax.dot_general(qh, k_ref[0, :, sl], (((1,), (1,)), ((), ())), preferred_element_type=F32)
        e = jnp.exp(s - jnp.max(s, axis=-1, keepdims=True))
        pv = jnp.dot(e.astype(BF16), v_ref[0, :, sl], preferred_element_type=F32)
        heads.append((pv / jnp.sum(e, axis=-1, keepdims=True)).astype(BF16))
    o = jnp.concatenate(heads, axis=-1)
    return x + jnp.dot(o, wo_ref[...], preferred_element_type=F32)


MLP_FF_CHUNK = 1024


def _mlp_body(x, g_ref, wu_ref, wd_ref):
    h = _rms(x, g_ref[...]).astype(BF16)
    ff = wu_ref.shape[1]
    ff_chunk = min(ff, MLP_FF_CHUNK)
    acc = x
    for c in range(ff // ff_chunk):
        sl = slice(c * ff_chunk, (c + 1) * ff_chunk)
        a = jnp.maximum(jnp.dot(h, wu_ref[:, sl], preferred_element_type=F32), 0.0)
        acc = acc + jnp.dot((a * a).astype(BF16), wd_ref[sl, :], preferred_element_type=F32)
    return acc


def _tail_kernel(x_ref, ys_ref, sb_ref, wg_ref, gs_ref, gb_ref, wo_ref,
                 gxa_ref, wq_ref, gq_ref, k_ref, v_ref, wxo_ref, gm_ref, wu_ref, wd_ref, o_ref):
    x = _mix_out_body(x_ref[0], ys_ref[0], sb_ref[0], wg_ref, gs_ref, gb_ref, wo_ref)
    x = _xattn_body(x, gxa_ref, wq_ref, gq_ref, k_ref, v_ref, wxo_ref)
    o_ref[0] = _mlp_body(x, gm_ref, wu_ref, wd_ref)


def _tail(x, y_s5, y_sb, w_glu, g_ssm, g_sb, w_out, g_xa, w_q, g_q, mk, mv, w_o, g_mlp, w_up, w_down):
    bsz, seq, d = x.shape
    ssm_w, sb_w = y_s5.shape[-1], y_sb.shape[-1]
    mlen, xa_w = mk.shape[1], mk.shape[2]
    tm = _token_tile(seq, 512)
    gq = (g_q.astype(F32) * (XA_HEAD_DIM ** -0.5))[None]
    tok = lambda w: pl.BlockSpec((1, tm, w), lambda b, i: (b, i, 0))
    per_b = pl.BlockSpec((1, mlen, xa_w), lambda b, i: (b, 0, 0))
    vec = lambda g: g.astype(F32)[None]
    return pl.pallas_call(
        _tail_kernel,
        grid=(bsz, seq // tm),
        in_specs=[tok(d), tok(ssm_w), tok(sb_w),
                  _const_spec(w_glu.shape), _const_spec((1, ssm_w)), _const_spec((1, sb_w)), _const_spec(w_out.shape),
                  _const_spec((1, d)), _const_spec(w_q.shape), _const_spec((1, XA_HEAD_DIM)), per_b, per_b,
                  _const_spec(w_o.shape),
                  _const_spec((1, d)), _const_spec(w_up.shape), _const_spec(w_down.shape)],
        out_specs=tok(d),
        out_shape=jax.ShapeDtypeStruct((bsz, seq, d), F32),
        compiler_params=_params("parallel", "parallel"),
        name="tail",
    )(x, y_s5, y_sb, w_glu.astype(BF16), vec(g_ssm), vec(g_sb), w_out.astype(BF16),
      vec(g_xa), w_q.astype(BF16), gq, mk, mv, w_o.astype(BF16),
      vec(g_mlp), w_up.astype(BF16), w_down.astype(BF16))


def kernel(x, mem, g_mix, w_in, ssm_a_re, ssm_a_im, ssm_log_dt, ssm_b_re, ssm_b_im, ssm_c_re, ssm_c_im,
           ssm_d, ssm_w_glu, sb_g_q, sb_g_k, g_out_ssm, g_out_sb, w_out, g_xa, g_mem, xa_w_q, xa_w_kv,
           xa_g_q, xa_g_k, xa_w_o, g_mlp, w_up, w_down):
    bsz, seq, d = x.shape
    n = bsz * seq
    ssm_w = ssm_d.shape[-1]
    sb_w = g_out_sb.shape[-1]
    x = x.astype(F32)
    for l in range(g_mix.shape[0]):
        u, q, k, v = _in_proj(x.reshape(n, d), g_mix[l], w_in[l], sb_g_q[l], sb_g_k[l], ssm_w, sb_w)
        mats = _s5_mats(ssm_a_re[l], ssm_a_im[l], ssm_log_dt[l], ssm_b_re[l], ssm_b_im[l],
                        ssm_c_re[l], ssm_c_im[l])
        y_s5 = _s5(u, mats, ssm_d[l], bsz)
        z_bound = (SB_HEAD_DIM * SB_Q_SCALE * SB_BOUND_SLACK
                   * jnp.max(jnp.abs(sb_g_q[l].astype(F32))) * jnp.max(jnp.abs(sb_g_k[l].astype(F32))))
        y_sb = _sb_attn(q.reshape(bsz, seq, sb_w), k.reshape(bsz, seq, sb_w), v.reshape(bsz, seq, sb_w), z_bound)
        mk, mv = _mem_kv(mem.astype(F32), g_mem[l], xa_w_kv[l], xa_g_k[l])
        x = _tail(x, y_s5.reshape(bsz, seq, ssm_w), y_sb, ssm_w_glu[l], g_out_ssm[l], g_out_sb[l], w_out[l],
                  g_xa[l], xa_w_q[l], xa_g_q[l], mk, mv, xa_w_o[l], g_mlp[l], w_up[l], w_down[l])
    return x
```

```python
import functools

import jax
import jax.numpy as jnp
from jax import lax
from jax.experimental import pallas as pl
from jax.experimental.pallas import tpu as pltpu

F32 = jnp.float32
BF16 = jnp.bfloat16

NORM_EPS = 1e-6
SSM_GROUP = 16
SSM_STATE = 64
SSM_CHUNK = 16
S5_COL_GROUPS = 4
SB_HEAD_DIM = 64
XA_HEADS = 4
XA_HEAD_DIM = 128
LANES = 128
LOG2E = 1.4426950408889634
SB_Q_SCALE = SB_HEAD_DIM ** -0.5 * LOG2E
SB_BOUND_SLACK = 1.05
VMEM_LIMIT_BYTES = 56 * 1024 * 1024


def _params(*sem):
    return pltpu.CompilerParams(dimension_semantics=sem, vmem_limit_bytes=VMEM_LIMIT_BYTES)


def _const_spec(shape):
    zeros = (0,) * len(shape)
    return pl.BlockSpec(shape, lambda *_: zeros, pipeline_mode=pl.Buffered(1))


def _rms(x, g):
    return x * lax.rsqrt(jnp.mean(x * x, axis=-1, keepdims=True) + NORM_EPS) * g


def _token_tile(n, want):
    t = min(n, want)
    assert n % t == 0
    return t


def _in_proj_kernel(x_ref, g_ref, w_ref, seg_ref, gq_ref, gk_ref,
                    u_ref, q_ref, k_ref, v_ref, *, ssm_w, sb_w):
    h = _rms(x_ref[...], g_ref[...])
    proj = jnp.dot(h.astype(BF16), w_ref[...], preferred_element_type=F32)
    u_ref[...] = proj[:, :ssm_w]
    q = proj[:, ssm_w:ssm_w + sb_w]
    k = proj[:, ssm_w + sb_w:ssm_w + 2 * sb_w]
    v_ref[...] = proj[:, ssm_w + 2 * sb_w:].astype(BF16)
    seg = seg_ref[...]

    def head_norm(t, g):
        ms = jnp.dot((t * t).astype(BF16), seg, preferred_element_type=F32)
        return t * lax.rsqrt(ms + NORM_EPS) * g

    q_ref[...] = head_norm(q, gq_ref[...]).astype(BF16)
    k_ref[...] = head_norm(k, gk_ref[...]).astype(BF16)


def _in_proj(x2, g_mix, w_in, g_q, g_k, ssm_w, sb_w):
    n, d = x2.shape
    tm = _token_tile(n, 512)
    heads = sb_w // SB_HEAD_DIM
    lane_head = jnp.arange(sb_w) // SB_HEAD_DIM
    seg = (lane_head[:, None] == lane_head[None, :]).astype(F32) / SB_HEAD_DIM
    gq = (jnp.tile(g_q.astype(F32), heads) * SB_Q_SCALE)[None]
    gk = jnp.tile(g_k.astype(F32), heads)[None]
    kern = functools.partial(_in_proj_kernel, ssm_w=ssm_w, sb_w=sb_w)
    tok = lambda w: pl.BlockSpec((tm, w), lambda i: (i, 0))
    return pl.pallas_call(
        kern,
        grid=(n // tm,),
        in_specs=[tok(d), _const_spec((1, d)), _const_spec(w_in.shape), _const_spec((sb_w, sb_w)),
                  _const_spec((1, sb_w)), _const_spec((1, sb_w))],
        out_specs=[tok(ssm_w), tok(sb_w), tok(sb_w), tok(sb_w)],
        out_shape=[jax.ShapeDtypeStruct((n, ssm_w), F32)] + [jax.ShapeDtypeStruct((n, sb_w), BF16)] * 3,
        compiler_params=_params("parallel"),
        name="in_proj",
    )(x2, g_mix.astype(F32)[None], w_in.astype(BF16), seg.astype(BF16), gq, gk)


def _s5_mats(a_re, a_im, log_dt, b_re, b_im, c_re, c_im):
    hp = lax.Precision.HIGHEST
    L = SSM_CHUNK
    g, p = a_re.shape
    nc = b_re.shape[-1]
    lr, li = a_re.astype(F32), a_im.astype(F32)
    dt = jnp.exp(log_dt.astype(F32))[:, None]
    n = jnp.arange(L + 1, dtype=F32)[:, None, None]
    mag = jnp.exp(n * (lr * dt)[None])
    pr, pi = mag * jnp.cos(n * (li * dt)[None]), mag * jnp.sin(n * (li * dt)[None])
    xr, xi = pr[1] - 1.0, pi[1]
    den = lr * lr + li * li
    fr, fi = (xr * lr + xi * li) / den, (xi * lr - xr * li) / den
    br, bi = b_re.astype(F32), b_im.astype(F32)
    bbr = fr[..., None] * br - fi[..., None] * bi
    bbi = fr[..., None] * bi + fi[..., None] * br
    cr, ci = c_re.astype(F32), c_im.astype(F32)
    mr = pr[..., None] * bbr[None] - pi[..., None] * bbi[None]
    mi = pr[..., None] * bbi[None] + pi[..., None] * bbr[None]
    kern = (jnp.einsum('gcp,ngpd->ngcd', cr, mr[:L], precision=hp)
            - jnp.einsum('gcp,ngpd->ngcd', ci, mi[:L], precision=hp))
    rev = L - 1 - jnp.arange(L)
    pcr = mr[rev].transpose(1, 0, 3, 2).reshape(g, L * nc, p)
    pci = mi[rev].transpose(1, 0, 3, 2).reshape(g, L * nc, p)
    p_mat = jnp.concatenate([pcr, pci], axis=-1)
    qr = cr[None] * pr[1:, :, None, :] - ci[None] * pi[1:, :, None, :]
    qi = cr[None] * pi[1:, :, None, :] + ci[None] * pr[1:, :, None, :]
    qr = qr.transpose(1, 3, 0, 2).reshape(g, p, L * nc)
    qi = qi.transpose(1, 3, 0, 2).reshape(g, p, L * nc)
    q_mat = jnp.concatenate([qr, -qi], axis=1)
    coef = jnp.stack([jnp.concatenate([pr[L], pr[L]], -1),
                      jnp.concatenate([-pi[L], pi[L]], -1)], axis=1)
    gt = LANES // nc
    nt = g // gt
    eye = jnp.eye(gt, dtype=F32)
    k5 = kern.reshape(L, nt, gt, nc, nc).transpose(1, 0, 2, 4, 3)
    bdk = k5[:, :, :, :, None, :] * eye[None, None, :, None, :, None]
    bdk = bdk.reshape(nt, L * LANES, LANES)
    pc = p_mat.reshape(nt, gt, L, nc, 2 * p).transpose(0, 2, 1, 3, 4).reshape(nt, L * LANES, 2 * p)
    qc = q_mat.reshape(nt, gt * 2 * p, L * nc)
    cf = coef.reshape(nt, gt, 2, 2 * p).transpose(0, 2, 1, 3).reshape(nt, 2, 2 * p * gt)
    col = jnp.arange(L * LANES)
    spread = (jnp.arange(L * nc)[:, None] == (col // LANES * nc + col % nc)[None, :])
    return bdk.astype(BF16), pc.astype(BF16), qc.astype(BF16), spread.astype(BF16), cf.astype(F32)


def _s5_kernel(u_ref, bdk_ref, pc_ref, qc_ref, spread_ref, coef_ref, d_ref, y_ref,
               tt_scr, pp_scr, qq_scr, e_scr, s_scr, *, sw, nc):
    L = SSM_CHUNK
    gt = LANES // nc
    lw = L * LANES

    @pl.when(pl.program_id(1) == 0)
    def _expand_tile_matrices():
        zero = jnp.zeros((LANES, LANES), BF16)
        for j in range(L):
            for i in range(L):
                lag = (i - j) * LANES
                tt_scr[j * LANES:(j + 1) * LANES, i * LANES:(i + 1) * LANES] = (
                    bdk_ref[0, lag:lag + LANES, :] if i >= j else zero)
        pc = pc_ref[0].astype(F32)
        row_g = lax.broadcasted_iota(jnp.int32, (L * LANES, LANES), 0) // nc % gt
        for g in range(gt):
            pp_scr[:, g * LANES:(g + 1) * LANES] = jnp.where(row_g == g, pc, 0.0).astype(BF16)
        full = jnp.dot(qc_ref[0], spread_ref[...], preferred_element_type=F32)
        rg = lax.broadcasted_iota(jnp.int32, full.shape, 0) // LANES
        cg = lax.broadcasted_iota(jnp.int32, full.shape, 1) // nc % gt
        qq_scr[...] = jnp.where(rg == cg, full, 0.0).astype(BF16)

    nk = u_ref.shape[0]
    u_steps = jnp.swapaxes(u_ref[...], 0, 1)
    u = jnp.concatenate([u_steps[t] for t in range(L)], axis=1).astype(BF16)
    e_s = jnp.dot(u, pp_scr[...], preferred_element_type=F32)
    e_scr[:, :sw] = e_s
    for g in range(gt):
        e_scr[:, sw + g * LANES:sw + (g + 1) * LANES] = pltpu.roll(
            e_s[:, g * LANES:(g + 1) * LANES], LANES // 2, axis=1)
    c1 = coef_ref[0, 0:1, :]
    c2 = coef_ref[0, 1:2, :]

    def step(k, carry):
        s, w = carry
        s_scr[pl.ds(k, 1), :] = s
        e = e_scr[pl.ds(k, 1), :]
        return s * c1 + w * c2 + e[:, :sw], w * c1 - s * c2 + e[:, sw:]

    zero = jnp.zeros((1, sw), F32)
    lax.fori_loop(0, nk, step, (zero, zero), unroll=8)
    cw = lw // S5_COL_GROUPS
    y = jnp.concatenate(
        [jnp.dot(u[:, :(n + 1) * cw], tt_scr[:(n + 1) * cw, n * cw:(n + 1) * cw], preferred_element_type=F32)
         for n in range(S5_COL_GROUPS)], axis=1)
    y += jnp.dot(s_scr[...].astype(BF16), qq_scr[...], preferred_element_type=F32)
    y_steps = jnp.stack([y[:, i * LANES:(i + 1) * LANES] for i in range(L)], axis=0)
    y_ref[...] = jnp.swapaxes(y_steps, 0, 1) + d_ref[...] * u_ref[...]


def _s5(u, mats, d_skip, bsz):
    bdk, pc, qc, spread, cf = mats
    n, width = u.shape
    L = SSM_CHUNK
    nt = width // LANES
    nk = n // L // bsz
    lw, sw = L * LANES, cf.shape[-1]
    nc = qc.shape[-1] // L
    kern = functools.partial(_s5_kernel, sw=sw, nc=nc)
    per_t = lambda a, b: pl.BlockSpec((1, a, b), lambda t, s: (t, 0, 0))
    seq_blk = pl.BlockSpec((nk, L, LANES), lambda t, s: (s, 0, t))
    y = pl.pallas_call(
        kern,
        grid=(nt, bsz),
        in_specs=[seq_blk, per_t(lw, LANES), per_t(lw, LANES), per_t(sw, L * nc),
                  _const_spec(spread.shape), per_t(2, sw), per_t(1, LANES)],
        out_specs=seq_blk,
        out_shape=jax.ShapeDtypeStruct((n // L, L, width), F32),
        scratch_shapes=[pltpu.VMEM((lw, lw), BF16), pltpu.VMEM((lw, sw), BF16), pltpu.VMEM((sw, lw), BF16),
                        pltpu.VMEM((nk, 2 * sw), F32), pltpu.VMEM((nk, sw), F32)],
        compiler_params=_params("parallel", "arbitrary"),
        name="s5_scan",
    )(u.reshape(n // L, L, width), bdk, pc, qc, spread, cf, d_skip.astype(F32).reshape(nt, 1, LANES))
    return y.reshape(n, width)


F32_ZERO_EXP2 = 160.0
SB_WAVE = 2


def _sb_attn_kernel(thr_ref, q_ref, k_ref, v_ref, tri_ref, o_ref, *, blk, nlb):
    i = pl.program_id(1)
    lane = lax.broadcasted_iota(jnp.int32, (blk, LANES), 1)
    first = lane < SB_HEAD_DIM
    qs = []
    for c in range(nlb):
        q = q_ref[0, :, c * LANES:(c + 1) * LANES]
        zq = jnp.zeros_like(q)
        qs.append(jnp.concatenate([jnp.where(first, q, zq), jnp.where(first, zq, q)], axis=0))
    tri = tri_ref[...]
    row = lax.broadcasted_iota(jnp.int32, (2 * blk, blk), 0)
    col = lax.broadcasted_iota(jnp.int32, (2 * blk, blk), 1)
    below = col < jnp.where(row >= blk, row - blk, row)

    def block(j, carry, acc, mask):
        start = pl.multiple_of(j * blk, blk)
        new_acc, new_carry = list(acc), list(carry)
        for c0 in range(0, nlb, SB_WAVE):
            wave = range(c0, min(nlb, c0 + SB_WAVE))
            zs, ps = {}, {}
            for c in wave:
                kb = k_ref[0, pl.ds(start, blk), c * LANES:(c + 1) * LANES]
                zs[c] = lax.dot_general(qs[c], kb, (((1,), (1,)), ((), ())), preferred_element_type=F32)
            for c in wave:
                z = zs[c]
                sp = jnp.maximum(z, 0.0) + jnp.log(1.0 + jnp.exp2(-jnp.abs(z))) * LOG2E
                if mask is not None:
                    sp = jnp.where(mask, sp, 0.0)
                    zs[c] = jnp.where(mask, z, -1e30)
                ps[c] = sp.astype(BF16)
            cums = {c: jnp.dot(ps[c], tri, preferred_element_type=F32) for c in wave}
            ws = {c: jnp.exp2(zs[c] - cums[c] - carry[c]).astype(BF16) for c in wave}
            for c in wave:
                vb = v_ref[0, pl.ds(start, blk), c * LANES:(c + 1) * LANES]
                new_acc[c] = acc[c] + jnp.dot(ws[c], vb, preferred_element_type=F32)
                new_carry[c] = carry[c] + cums[c][:, 0:1]
        return tuple(new_carry), tuple(new_acc)

    zc = tuple(jnp.zeros((2 * blk, 1), F32) for _ in range(nlb))
    za = tuple(jnp.zeros((2 * blk, LANES), F32) for _ in range(nlb))
    carry, acc = block(i, zc, za, below)
    no_prev = jnp.where(i == 0, jnp.float32(1e30), jnp.float32(0.0))
    carry, acc = block(jnp.maximum(i - 1, 0), tuple(cv + no_prev for cv in carry), acc, None)
    thr = thr_ref[0]

    def row_min(carry):
        return jnp.min(functools.reduce(jnp.minimum, carry))

    def more(st):
        return jnp.logical_and(st[0] >= 0, st[1] <= thr)

    def step(st):
        j, _, carry, acc = st
        carry, acc = block(j, carry, acc, None)
        return j - 1, row_min(carry), carry, acc

    _, _, carry, acc = lax.while_loop(more, step, (i - 2, row_min(carry), carry, acc))
    for c in range(nlb):
        o_ref[0, :, c * LANES:(c + 1) * LANES] = jnp.where(first, acc[c][:blk], acc[c][blk:]).astype(o_ref.dtype)


def _sb_attn(q, k, v, z_bound):
    bsz, seq, width = q.shape
    blk = _token_tile(seq, 256)
    assert width % LANES == 0
    tri = (jnp.arange(blk)[:, None] >= jnp.arange(blk)[None, :]).astype(BF16)
    thr = (z_bound + F32_ZERO_EXP2).astype(F32).reshape(1)
    kern = functools.partial(_sb_attn_kernel, blk=blk, nlb=width // LANES)
    q_blk = pl.BlockSpec((1, blk, width), lambda b, i: (b, i, 0))
    whole_seq = pl.BlockSpec((1, seq, width), lambda b, i: (b, 0, 0), pipeline_mode=pl.Buffered(1))
    return pl.pallas_call(
        kern,
        grid=(bsz, seq // blk),
        in_specs=[pl.BlockSpec(memory_space=pltpu.SMEM), q_blk, whole_seq, whole_seq,
                  _const_spec((blk, blk))],
        out_specs=q_blk,
        out_shape=jax.ShapeDtypeStruct((bsz, seq, width), BF16),
        compiler_params=_params("parallel", "arbitrary"),
        name="sb_attn",
    )(thr, q, k, v, tri)


def _mix_out_body(x, ys, sb, wg_ref, gs_ref, gb_ref, wo_ref):
    ssm_w = ys.shape[1]
    y = jax.nn.gelu(ys)
    gate = jnp.dot(y.astype(BF16), wg_ref[...], preferred_element_type=F32)
    y = y * jax.nn.sigmoid(gate)
    yn = _rms(y, gs_ref[...]).astype(BF16)
    sn = _rms(sb.astype(F32), gb_ref[...]).astype(BF16)
    out = jnp.dot(yn, wo_ref[:ssm_w, :], preferred_element_type=F32)
    out += jnp.dot(sn, wo_ref[ssm_w:, :], preferred_element_type=F32)
    return x + out


def _mem_kv_kernel(m_ref, g_ref, w_ref, gk_ref, k_ref, v_ref, *, xa_w):
    h = _rms(m_ref[0], g_ref[...])
    kv = jnp.dot(h.astype(BF16), w_ref[...], preferred_element_type=F32)
    for hd in range(xa_w // XA_HEAD_DIM):
        sl = slice(hd * XA_HEAD_DIM, (hd + 1) * XA_HEAD_DIM)
        k_ref[0, :, sl] = _rms(kv[:, sl], gk_ref[...]).astype(BF16)
    v_ref[0] = kv[:, xa_w:].astype(BF16)


def _mem_kv(mem, g_mem, w_kv, g_k):
    bsz, mlen, d = mem.shape
    xa_w = w_kv.shape[1] // 2
    kern = functools.partial(_mem_kv_kernel, xa_w=xa_w)
    per_b = lambda w: pl.BlockSpec((1, mlen, w), lambda b: (b, 0, 0))
    return pl.pallas_call(
        kern,
        grid=(bsz,),
        in_specs=[per_b(d), _const_spec((1, d)), _const_spec(w_kv.shape), _const_spec((1, XA_HEAD_DIM))],
        out_specs=[per_b(xa_w), per_b(xa_w)],
        out_shape=[jax.ShapeDtypeStruct((bsz, mlen, xa_w), BF16)] * 2,
        compiler_params=_params("parallel"),
        name="mem_kv",
    )(mem, g_mem.astype(F32)[None], w_kv.astype(BF16), g_k.astype(F32)[None])


def _xattn_body(x, g_ref, wq_ref, gq_ref, k_ref, v_ref, wo_ref):
    h = _rms(x, g_ref[...])
    q = jnp.dot(h.astype(BF16), wq_ref[...], preferred_element_type=F32)
    heads = []
    for hd in range(q.shape[1] // XA_HEAD_DIM):
        sl = slice(hd * XA_HEAD_DIM, (hd + 1) * XA_HEAD_DIM)
        qh = _rms(q[:, sl], gq_ref[...]).astype(BF16)
        s = lax.dot_general(qh, k_ref[0, :, sl], (((1,), (1,)), ((), ())), preferred_element_type=F32)
        e = jnp.exp(s - jnp.max(s, axis=-1, keepdims=True))
        pv = jnp.dot(e.astype(BF16), v_ref[0, :, sl], preferred_element_type=F32)
        heads.append((pv / jnp.sum(e, axis=-1, keepdims=True)).astype(BF16))
    o = jnp.concatenate(heads, axis=-1)
    return x + jnp.dot(o, wo_ref[...], preferred_element_type=F32)


MLP_FF_CHUNK = 1024


def _mlp_body(x, g_ref, wu_ref, wd_ref):
    h = _rms(x, g_ref[...]).astype(BF16)
    ff = wu_ref.shape[1]
    ff_chunk = min(ff, MLP_FF_CHUNK)
    acc = x
    for c in range(ff // ff_chunk):
        sl = slice(c * ff_chunk, (c + 1) * ff_chunk)
        a = jnp.maximum(jnp.dot(h, wu_ref[:, sl], preferred_element_type=F32), 0.0)
        acc = acc + jnp.dot((a * a).astype(BF16), wd_ref[sl, :], preferred_element_type=F32)
    return acc


def _tail_kernel(x_ref, ys_ref, sb_ref, wg_ref, gs_ref, gb_ref, wo_ref,
                 gxa_ref, wq_ref, gq_ref, k_ref, v_ref, wxo_ref, gm_ref, wu_ref, wd_ref, o_ref):
    x = _mix_out_body(x_ref[0], ys_ref[0], sb_ref[0], wg_ref, gs_ref, gb_ref, wo_ref)
    x = _xattn_body(x, gxa_ref, wq_ref, gq_ref, k_ref, v_ref, wxo_ref)
    o_ref[0] = _mlp_body(x, gm_ref, wu_ref, wd_ref)


def _tail(x, y_s5, y_sb, w_glu, g_ssm, g_sb, w_out, g_xa, w_q, g_q, mk, mv, w_o, g_mlp, w_up, w_down):
    bsz, seq, d = x.shape
    ssm_w, sb_w = y_s5.shape[-1], y_sb.shape[-1]
    mlen, xa_w = mk.shape[1], mk.shape[2]
    tm = _token_tile(seq, 1024)
    gq = (g_q.astype(F32) * (XA_HEAD_DIM ** -0.5))[None]
    tok = lambda w: pl.BlockSpec((1, tm, w), lambda b, i: (b, i, 0))
    per_b = pl.BlockSpec((1, mlen, xa_w), lambda b, i: (b, 0, 0))
    vec = lambda g: g.astype(F32)[None]
    return pl.pallas_call(
        _tail_kernel,
        grid=(bsz, seq // tm),
        in_specs=[tok(d), tok(ssm_w), tok(sb_w),
                  _const_spec(w_glu.shape), _const_spec((1, ssm_w)), _const_spec((1, sb_w)), _const_spec(w_out.shape),
                  _const_spec((1, d)), _const_spec(w_q.shape), _const_spec((1, XA_HEAD_DIM)), per_b, per_b,
                  _const_spec(w_o.shape),
                  _const_spec((1, d)), _const_spec(w_up.shape), _const_spec(w_down.shape)],
        out_specs=tok(d),
        out_shape=jax.ShapeDtypeStruct((bsz, seq, d), F32),
        compiler_params=_params("parallel", "parallel"),
        name="tail",
    )(x, y_s5, y_sb, w_glu.astype(BF16), vec(g_ssm), vec(g_sb), w_out.astype(BF16),
      vec(g_xa), w_q.astype(BF16), gq, mk, mv, w_o.astype(BF16),
      vec(g_mlp), w_up.astype(BF16), w_down.astype(BF16))


def kernel(x, mem, g_mix, w_in, ssm_a_re, ssm_a_im, ssm_log_dt, ssm_b_re, ssm_b_im, ssm_c_re, ssm_c_im,
           ssm_d, ssm_w_glu, sb_g_q, sb_g_k, g_out_ssm, g_out_sb, w_out, g_xa, g_mem, xa_w_q, xa_w_kv,
           xa_g_q, xa_g_k, xa_w_o, g_mlp, w_up, w_down):
    bsz, seq, d = x.shape
    n = bsz * seq
    ssm_w = ssm_d.shape[-1]
    sb_w = g_out_sb.shape[-1]
    x = x.astype(F32)
    for l in range(g_mix.shape[0]):
        u, q, k, v = _in_proj(x.reshape(n, d), g_mix[l], w_in[l], sb_g_q[l], sb_g_k[l], ssm_w, sb_w)
        mats = _s5_mats(ssm_a_re[l], ssm_a_im[l], ssm_log_dt[l], ssm_b_re[l], ssm_b_im[l],
                        ssm_c_re[l], ssm_c_im[l])
        y_s5 = _s5(u, mats, ssm_d[l], bsz)
        z_bound = (SB_HEAD_DIM * SB_Q_SCALE * SB_BOUND_SLACK
                   * jnp.max(jnp.abs(sb_g_q[l].astype(F32))) * jnp.max(jnp.abs(sb_g_k[l].astype(F32))))
        y_sb = _sb_attn(q.reshape(bsz, seq, sb_w), k.reshape(bsz, seq, sb_w), v.reshape(bsz, seq, sb_w), z_bound)
        mk, mv = _mem_kv(mem.astype(F32), g_mem[l], xa_w_kv[l], xa_g_k[l])
        x = _tail(x, y_s5.reshape(bsz, seq, ssm_w), y_sb, ssm_w_glu[l], g_out_ssm[l], g_out_sb[l], w_out[l],
                  g_xa[l], xa_w_q[l], xa_g_q[l], mk, mv, xa_w_o[l], g_mlp[l], w_up[l], w_down[l])
    return x
```

```python
import functools

import jax
import numpy as np
import jax.numpy as jnp
from jax import lax
from jax.experimental import pallas as pl
from jax.experimental.pallas import tpu as pltpu

F32 = jnp.float32
BF16 = jnp.bfloat16

NORM_EPS = 1e-6
SSM_GROUP = 16
SSM_STATE = 64
SSM_CHUNK = 16
S5_COL_GROUPS = 4
SB_HEAD_DIM = 64
XA_HEADS = 4
XA_HEAD_DIM = 128
LANES = 128
LOG2E = 1.4426950408889634
SB_Q_SCALE = SB_HEAD_DIM ** -0.5 * LOG2E
SB_BOUND_SLACK = 1.05
VMEM_LIMIT_BYTES = 56 * 1024 * 1024


def _params(*sem):
    return pltpu.CompilerParams(dimension_semantics=sem, vmem_limit_bytes=VMEM_LIMIT_BYTES)


def _const_spec(shape):
    zeros = (0,) * len(shape)
    return pl.BlockSpec(shape, lambda *_: zeros, pipeline_mode=pl.Buffered(1))


def _rms(x, g):
    return x * lax.rsqrt(jnp.mean(x * x, axis=-1, keepdims=True) + NORM_EPS) * g


def _token_tile(n, want):
    t = min(n, want)
    assert n % t == 0
    return t


def _in_proj_kernel(x_ref, g_ref, w_ref, seg_ref, gq_ref, gk_ref,
                    u_ref, q_ref, k_ref, v_ref, *, ssm_w, sb_w):
    h = _rms(x_ref[...], g_ref[...])
    proj = jnp.dot(h.astype(BF16), w_ref[...], preferred_element_type=F32)
    u_ref[...] = proj[:, :ssm_w]
    q = proj[:, ssm_w:ssm_w + sb_w]
    k = proj[:, ssm_w + sb_w:ssm_w + 2 * sb_w]
    v_ref[...] = proj[:, ssm_w + 2 * sb_w:].astype(BF16)
    seg = seg_ref[...]

    def head_norm(t, g):
        ms = jnp.dot((t * t).astype(BF16), seg, preferred_element_type=F32)
        return t * lax.rsqrt(ms + NORM_EPS) * g

    q_ref[...] = head_norm(q, gq_ref[...]).astype(BF16)
    k_ref[...] = head_norm(k, gk_ref[...]).astype(BF16)


def _in_proj(x2, g_mix, w_in, g_q, g_k, ssm_w, sb_w):
    n, d = x2.shape
    tm = _token_tile(n, 1024)
    heads = sb_w // SB_HEAD_DIM
    lane_head = np.arange(sb_w) // SB_HEAD_DIM
    seg = jnp.asarray((lane_head[:, None] == lane_head[None, :]) / SB_HEAD_DIM, BF16)
    gq = (jnp.tile(g_q.astype(F32), heads) * SB_Q_SCALE)[None]
    gk = jnp.tile(g_k.astype(F32), heads)[None]
    kern = functools.partial(_in_proj_kernel, ssm_w=ssm_w, sb_w=sb_w)
    tok = lambda w: pl.BlockSpec((tm, w), lambda i: (i, 0))
    return pl.pallas_call(
        kern,
        grid=(n // tm,),
        in_specs=[tok(d), _const_spec((1, d)), _const_spec(w_in.shape), _const_spec((sb_w, sb_w)),
                  _const_spec((1, sb_w)), _const_spec((1, sb_w))],
        out_specs=[tok(ssm_w), tok(sb_w), tok(sb_w), tok(sb_w)],
        out_shape=[jax.ShapeDtypeStruct((n, ssm_w), F32)] + [jax.ShapeDtypeStruct((n, sb_w), BF16)] * 3,
        compiler_params=_params("parallel"),
        name="in_proj",
    )(x2, g_mix.astype(F32)[None], w_in.astype(BF16), seg, gq, gk)


def _s5_mats(a_re, a_im, log_dt, b_re, b_im, c_re, c_im):
    hp = lax.Precision.HIGHEST
    L = SSM_CHUNK
    g, p = a_re.shape
    nc = b_re.shape[-1]
    gt = LANES // nc
    nt = g // gt
    lr, li = a_re.astype(F32), a_im.astype(F32)
    dt = jnp.exp(log_dt.astype(F32))[:, None]
    n = jnp.arange(L + 1, dtype=F32)[:, None, None]
    mag = jnp.exp(n * (lr * dt)[None])
    pr, pi = mag * jnp.cos(n * (li * dt)[None]), mag * jnp.sin(n * (li * dt)[None])
    xr, xi = pr[1] - 1.0, pi[1]
    den = lr * lr + li * li
    fr, fi = (xr * lr + xi * li) / den, (xi * lr - xr * li) / den
    brt = jnp.swapaxes(b_re.astype(F32), 1, 2)
    bit = jnp.swapaxes(b_im.astype(F32), 1, 2)
    bbr = fr[:, None, :] * brt - fi[:, None, :] * bit
    bbi = fr[:, None, :] * bit + fi[:, None, :] * brt
    cr, ci = c_re.astype(F32), c_im.astype(F32)
    mr = pr[:, :, None, :] * bbr[None] - pi[:, :, None, :] * bbi[None]
    mi = pr[:, :, None, :] * bbi[None] + pi[:, :, None, :] * bbr[None]
    taps = (jnp.einsum('ngdp,gcp->ngdc', mr[:L], cr, precision=hp)
            - jnp.einsum('ngdp,gcp->ngdc', mi[:L], ci, precision=hp))
    tile_rows = lambda a: a.reshape((L, nt, gt) + a.shape[2:])
    kt = tile_rows(taps).transpose(1, 0, 2, 3, 4).reshape(nt, L * LANES, nc)
    rev = L - 1 - jnp.arange(L)
    pcat = jnp.concatenate([mr[rev], mi[rev]], axis=-1)
    pc = tile_rows(pcat).transpose(1, 0, 2, 3, 4).reshape(nt, L * LANES, 2 * p)
    qr = cr[None] * pr[1:, :, None, :] - ci[None] * pi[1:, :, None, :]
    qi = cr[None] * pi[1:, :, None, :] + ci[None] * pr[1:, :, None, :]
    qcat = jnp.concatenate([qr, -qi], axis=-1)
    qct = tile_rows(qcat).transpose(1, 0, 3, 2, 4).reshape(nt, L * nc, gt * 2 * p)
    coef = jnp.stack([jnp.concatenate([pr[L], pr[L]], -1),
                      jnp.concatenate([-pi[L], pi[L]], -1)], axis=1)
    cf = coef.reshape(nt, gt, 2, 2 * p).transpose(0, 2, 1, 3).reshape(nt, 2, 2 * p * gt)
    return kt.astype(BF16), pc.astype(BF16), qct.astype(BF16), cf.astype(F32)


def _spread_consts(nc):
    L = SSM_CHUNK
    lane = np.arange(LANES)
    to_tile = (np.arange(nc)[:, None] == (lane % nc)[None, :])
    col = np.arange(L * LANES)
    to_row = (np.arange(L * nc)[:, None] == (col // LANES * nc + col % nc)[None, :])
    return jnp.asarray(to_tile, BF16), jnp.asarray(to_row, BF16)


def _s5_kernel(u_ref, kt_ref, pc_ref, qct_ref, to_tile_ref, to_row_ref, coef_ref, d_ref, y_ref,
               tt_scr, pp_scr, qq_scr, e_scr, s_scr, *, sw, nc):
    L = SSM_CHUNK
    gt = LANES // nc
    lw = L * LANES

    @pl.when(pl.program_id(1) == 0)
    def _expand_tile_matrices():
        row_g = lax.broadcasted_iota(jnp.int32, (L * LANES, LANES), 0) // nc % gt
        col_g = lax.broadcasted_iota(jnp.int32, (L * LANES, LANES), 1) // nc
        taps = jnp.dot(kt_ref[0], to_tile_ref[...], preferred_element_type=F32)
        taps = jnp.where(row_g == col_g, taps, 0.0).astype(BF16)
        zero = jnp.zeros((LANES, LANES), BF16)
        for j in range(L):
            for i in range(L):
                lag = (i - j) * LANES
                tt_scr[j * LANES:(j + 1) * LANES, i * LANES:(i + 1) * LANES] = (
                    taps[lag:lag + LANES, :] if i >= j else zero)
        pc = pc_ref[0].astype(F32)
        for g in range(gt):
            pp_scr[:, g * LANES:(g + 1) * LANES] = jnp.where(row_g == g, pc, 0.0).astype(BF16)
        full = lax.dot_general(qct_ref[0], to_row_ref[...], (((0,), (0,)), ((), ())), preferred_element_type=F32)
        rg = lax.broadcasted_iota(jnp.int32, full.shape, 0) // LANES
        cg = lax.broadcasted_iota(jnp.int32, full.shape, 1) // nc % gt
        qq_scr[...] = jnp.where(rg == cg, full, 0.0).astype(BF16)

    nk = u_ref.shape[0]
    u_steps = jnp.swapaxes(u_ref[...], 0, 1)
    u = jnp.concatenate([u_steps[t] for t in range(L)], axis=1).astype(BF16)
    e_s = jnp.dot(u, pp_scr[...], preferred_element_type=F32)
    e_scr[:, :sw] = e_s
    for g in range(gt):
        e_scr[:, sw + g * LANES:sw + (g + 1) * LANES] = pltpu.roll(
            e_s[:, g * LANES:(g + 1) * LANES], LANES // 2, axis=1)
    c1 = coef_ref[0, 0:1, :]
    c2 = coef_ref[0, 1:2, :]

    def step(k, carry):
        s, w = carry
        s_scr[pl.ds(k, 1), :] = s
        e = e_scr[pl.ds(k, 1), :]
        return s * c1 + w * c2 + e[:, :sw], w * c1 - s * c2 + e[:, sw:]

    zero = jnp.zeros((1, sw), F32)
    lax.fori_loop(0, nk, step, (zero, zero), unroll=8)
    cw = lw // S5_COL_GROUPS
    y = jnp.concatenate(
        [jnp.dot(u[:, :(n + 1) * cw], tt_scr[:(n + 1) * cw, n * cw:(n + 1) * cw], preferred_element_type=F32)
         for n in range(S5_COL_GROUPS)], axis=1)
    y += jnp.dot(s_scr[...].astype(BF16), qq_scr[...], preferred_element_type=F32)
    y_steps = jnp.stack([y[:, i * LANES:(i + 1) * LANES] for i in range(L)], axis=0)
    y_ref[...] = jnp.swapaxes(y_steps, 0, 1) + d_ref[...] * u_ref[...]


def _s5(u, mats, d_skip, bsz):
    kt, pc, qct, cf = mats
    n, width = u.shape
    L = SSM_CHUNK
    nt = width // LANES
    nk = n // L // bsz
    lw, sw = L * LANES, cf.shape[-1]
    nc = kt.shape[-1]
    to_tile, to_row = _spread_consts(nc)
    kern = functools.partial(_s5_kernel, sw=sw, nc=nc)
    per_t = lambda a, b: pl.BlockSpec((1, a, b), lambda t, s: (t, 0, 0))
    seq_blk = pl.BlockSpec((nk, L, LANES), lambda t, s: (s, 0, t))
    y = pl.pallas_call(
        kern,
        grid=(nt, bsz),
        in_specs=[seq_blk, per_t(lw, nc), per_t(lw, LANES), per_t(L * nc, sw),
                  _const_spec(to_tile.shape), _const_spec(to_row.shape), per_t(2, sw), per_t(1, LANES)],
        out_specs=seq_blk,
        out_shape=jax.ShapeDtypeStruct((n // L, L, width), F32),
        scratch_shapes=[pltpu.VMEM((lw, lw), BF16), pltpu.VMEM((lw, sw), BF16), pltpu.VMEM((sw, lw), BF16),
                        pltpu.VMEM((nk, 2 * sw), F32), pltpu.VMEM((nk, sw), F32)],
        compiler_params=_params("parallel", "arbitrary"),
        name="s5_scan",
    )(u.reshape(n // L, L, width), kt, pc, qct, to_tile, to_row, cf, d_skip.astype(F32).reshape(nt, 1, LANES))
    return y.reshape(n, width)


F32_ZERO_EXP2 = 160.0
SB_WAVE = 2


def _sb_attn_kernel(thr_ref, q_ref, k_ref, v_ref, tri_ref, o_ref, *, blk, nlb):
    i = pl.program_id(1)
    lane = lax.broadcasted_iota(jnp.int32, (blk, LANES), 1)
    first = lane < SB_HEAD_DIM
    qs = []
    for c in range(nlb):
        q = q_ref[0, :, c * LANES:(c + 1) * LANES]
        zq = jnp.zeros_like(q)
        qs.append(jnp.concatenate([jnp.where(first, q, zq), jnp.where(first, zq, q)], axis=0))
    tri = tri_ref[...]
    row = lax.broadcasted_iota(jnp.int32, (2 * blk, blk), 0)
    col = lax.broadcasted_iota(jnp.int32, (2 * blk, blk), 1)
    below = col < jnp.where(row >= blk, row - blk, row)

    def block(j, carry, acc, mask):
        start = pl.multiple_of(j * blk, blk)
        new_acc, new_carry = list(acc), list(carry)
        for c0 in range(0, nlb, SB_WAVE):
            wave = range(c0, min(nlb, c0 + SB_WAVE))
            zs, ps = {}, {}
            for c in wave:
                kb = k_ref[0, pl.ds(start, blk), c * LANES:(c + 1) * LANES]
                zs[c] = lax.dot_general(qs[c], kb, (((1,), (1,)), ((), ())), preferred_element_type=F32)
            for c in wave:
                z = zs[c]
                sp = jnp.maximum(z, 0.0) + jnp.log(1.0 + jnp.exp2(-jnp.abs(z))) * LOG2E
                if mask is not None:
                    sp = jnp.where(mask, sp, 0.0)
                    zs[c] = jnp.where(mask, z, -1e30)
                ps[c] = sp.astype(BF16)
            cums = {c: jnp.dot(ps[c], tri, preferred_element_type=F32) for c in wave}
            ws = {c: jnp.exp2(zs[c] - cums[c] - carry[c]).astype(BF16) for c in wave}
            for c in wave:
                vb = v_ref[0, pl.ds(start, blk), c * LANES:(c + 1) * LANES]
                new_acc[c] = acc[c] + jnp.dot(ws[c], vb, preferred_element_type=F32)
                new_carry[c] = carry[c] + cums[c][:, 0:1]
        return tuple(new_carry), tuple(new_acc)

    zc = tuple(jnp.zeros((2 * blk, 1), F32) for _ in range(nlb))
    za = tuple(jnp.zeros((2 * blk, LANES), F32) for _ in range(nlb))
    carry, acc = block(i, zc, za, below)
    no_prev = jnp.where(i == 0, jnp.float32(1e30), jnp.float32(0.0))
    carry, acc = block(jnp.maximum(i - 1, 0), tuple(cv + no_prev for cv in carry), acc, None)
    thr = thr_ref[0]

    def row_min(carry):
        return jnp.min(functools.reduce(jnp.minimum, carry))

    def more(st):
        return jnp.logical_and(st[0] >= 0, st[1] <= thr)

    def step(st):
        j, _, carry, acc = st
        carry, acc = block(j, carry, acc, None)
        return j - 1, row_min(carry), carry, acc

    _, _, carry, acc = lax.while_loop(more, step, (i - 2, row_min(carry), carry, acc))
    for c in range(nlb):
        o_ref[0, :, c * LANES:(c + 1) * LANES] = jnp.where(first, acc[c][:blk], acc[c][blk:]).astype(o_ref.dtype)


def _sb_attn(q, k, v, z_bound):
    bsz, seq, width = q.shape
    blk = _token_tile(seq, 256)
    assert width % LANES == 0
    tri = jnp.asarray(np.arange(blk)[:, None] >= np.arange(blk)[None, :], BF16)
    thr = (z_bound + F32_ZERO_EXP2).astype(F32).reshape(1)
    kern = functools.partial(_sb_attn_kernel, blk=blk, nlb=width // LANES)
    q_blk = pl.BlockSpec((1, blk, width), lambda b, i: (b, i, 0))
    whole_seq = pl.BlockSpec((1, seq, width), lambda b, i: (b, 0, 0), pipeline_mode=pl.Buffered(1))
    return pl.pallas_call(
        kern,
        grid=(bsz, seq // blk),
        in_specs=[pl.BlockSpec(memory_space=pltpu.SMEM), q_blk, whole_seq, whole_seq,
                  _const_spec((blk, blk))],
        out_specs=q_blk,
        out_shape=jax.ShapeDtypeStruct((bsz, seq, width), BF16),
        compiler_params=_params("parallel", "arbitrary"),
        name="sb_attn",
    )(thr, q, k, v, tri)


def _mix_out_body(x, ys, sb, wg_ref, gs_ref, gb_ref, wo_ref):
    ssm_w = ys.shape[1]
    y = jax.nn.gelu(ys)
    gate = jnp.dot(y.astype(BF16), wg_ref[...], preferred_element_type=F32)
    y = y * jax.nn.sigmoid(gate)
    yn = _rms(y, gs_ref[...]).astype(BF16)
    sn = _rms(sb.astype(F32), gb_ref[...]).astype(BF16)
    out = jnp.dot(yn, wo_ref[:ssm_w, :], preferred_element_type=F32)
    out += jnp.dot(sn, wo_ref[ssm_w:, :], preferred_element_type=F32)
    return x + out


def _mem_kv_kernel(m_ref, g_ref, w_ref, gk_ref, k_ref, v_ref, *, xa_w):
    h = _rms(m_ref[0], g_ref[...])
    kv = jnp.dot(h.astype(BF16), w_ref[...], preferred_element_type=F32)
    for hd in range(xa_w // XA_HEAD_DIM):
        sl = slice(hd * XA_HEAD_DIM, (hd + 1) * XA_HEAD_DIM)
        k_ref[0, :, sl] = _rms(kv[:, sl], gk_ref[...]).astype(BF16)
    v_ref[0] = kv[:, xa_w:].astype(BF16)


def _mem_kv(mem, g_mem, w_kv, g_k):
    bsz, mlen, d = mem.shape
    xa_w = w_kv.shape[1] // 2
    kern = functools.partial(_mem_kv_kernel, xa_w=xa_w)
    per_b = lambda w: pl.BlockSpec((1, mlen, w), lambda b: (b, 0, 0))
    return pl.pallas_call(
        kern,
        grid=(bsz,),
        in_specs=[per_b(d), _const_spec((1, d)), _const_spec(w_kv.shape), _const_spec((1, XA_HEAD_DIM))],
        out_specs=[per_b(xa_w), per_b(xa_w)],
        out_shape=[jax.ShapeDtypeStruct((bsz, mlen, xa_w), BF16)] * 2,
        compiler_params=_params("parallel"),
        name="mem_kv",
    )(mem, g_mem.astype(F32)[None], w_kv.astype(BF16), g_k.astype(F32)[None])


def _xattn_body(x, g_ref, wq_ref, gq_ref, k_ref, v_ref, wo_ref):
    h = _rms(x, g_ref[...])
    q = jnp.dot(h.astype(BF16), wq_ref[...], preferred_element_type=F32)
    heads = []
    for hd in range(q.shape[1] // XA_HEAD_DIM):
        sl = slice(hd * XA_HEAD_DIM, (hd + 1) * XA_HEAD_DIM)
        qh = _rms(q[:, sl], gq_ref[...]).astype(BF16)
        s = lax.dot_general(qh, k_ref[0, :, sl], (((1,), (1,)), ((), ())), preferred_element_type=F32)
        e = jnp.exp(s - jnp.max(s, axis=-1, keepdims=True))
        pv = jnp.dot(e.astype(BF16), v_ref[0, :, sl], preferred_element_type=F32)
        heads.append((pv / jnp.sum(e, axis=-1, keepdims=True)).astype(BF16))
    o = jnp.concatenate(heads, axis=-1)
    return x + jnp.dot(o, wo_ref[...], preferred_element_type=F32)


MLP_FF_CHUNK = 1024


def _mlp_body(x, g_ref, wu_ref, wd_ref):
    h = _rms(x, g_ref[...]).astype(BF16)
    ff = wu_ref.shape[1]
    ff_chunk = min(ff, MLP_FF_CHUNK)
    acc = x
    for c in range(ff // ff_chunk):
        sl = slice(c * ff_chunk, (c + 1) * ff_chunk)
        a = jnp.maximum(jnp.dot(h, wu_ref[:, sl], preferred_element_type=F32), 0.0)
        acc = acc + jnp.dot((a * a).astype(BF16), wd_ref[sl, :], preferred_element_type=F32)
    return acc


def _tail_kernel(x_ref, ys_ref, sb_ref, wg_ref, gs_ref, gb_ref, wo_ref,
                 gxa_ref, wq_ref, gq_ref, k_ref, v_ref, wxo_ref, gm_ref, wu_ref, wd_ref, o_ref):
    x = _mix_out_body(x_ref[0], ys_ref[0], sb_ref[0], wg_ref, gs_ref, gb_ref, wo_ref)
    x = _xattn_body(x, gxa_ref, wq_ref, gq_ref, k_ref, v_ref, wxo_ref)
    o_ref[0] = _mlp_body(x, gm_ref, wu_ref, wd_ref)


def _tail(x, y_s5, y_sb, w_glu, g_ssm, g_sb, w_out, g_xa, w_q, g_q, mk, mv, w_o, g_mlp, w_up, w_down):
    bsz, seq, d = x.shape
    ssm_w, sb_w = y_s5.shape[-1], y_sb.shape[-1]
    mlen, xa_w = mk.shape[1], mk.shape[2]
    tm = _token_tile(seq, 1024)
    gq = (g_q.astype(F32) * (XA_HEAD_DIM ** -0.5))[None]
    tok = lambda w: pl.BlockSpec((1, tm, w), lambda b, i: (b, i, 0))
    per_b = pl.BlockSpec((1, mlen, xa_w), lambda b, i: (b, 0, 0))
    vec = lambda g: g.astype(F32)[None]
    return pl.pallas_call(
        _tail_kernel,
        grid=(bsz, seq // tm),
        in_specs=[tok(d), tok(ssm_w), tok(sb_w),
                  _const_spec(w_glu.shape), _const_spec((1, ssm_w)), _const_spec((1, sb_w)), _const_spec(w_out.shape),
                  _const_spec((1, d)), _const_spec(w_q.shape), _const_spec((1, XA_HEAD_DIM)), per_b, per_b,
                  _const_spec(w_o.shape),
                  _const_spec((1, d)), _const_spec(w_up.shape), _const_spec(w_down.shape)],
        out_specs=tok(d),
        out_shape=jax.ShapeDtypeStruct((bsz, seq, d), F32),
        compiler_params=_params("parallel", "parallel"),
        name="tail",
    )(x, y_s5, y_sb, w_glu.astype(BF16), vec(g_ssm), vec(g_sb), w_out.astype(BF16),
      vec(g_xa), w_q.astype(BF16), gq, mk, mv, w_o.astype(BF16),
      vec(g_mlp), w_up.astype(BF16), w_down.astype(BF16))


def kernel(x, mem, g_mix, w_in, ssm_a_re, ssm_a_im, ssm_log_dt, ssm_b_re, ssm_b_im, ssm_c_re, ssm_c_im,
           ssm_d, ssm_w_glu, sb_g_q, sb_g_k, g_out_ssm, g_out_sb, w_out, g_xa, g_mem, xa_w_q, xa_w_kv,
           xa_g_q, xa_g_k, xa_w_o, g_mlp, w_up, w_down):
    bsz, seq, d = x.shape
    n = bsz * seq
    ssm_w = ssm_d.shape[-1]
    sb_w = g_out_sb.shape[-1]
    x = x.astype(F32)
    for l in range(g_mix.shape[0]):
        u, q, k, v = _in_proj(x.reshape(n, d), g_mix[l], w_in[l], sb_g_q[l], sb_g_k[l], ssm_w, sb_w)
        mats = _s5_mats(ssm_a_re[l], ssm_a_im[l], ssm_log_dt[l], ssm_b_re[l], ssm_b_im[l],
                        ssm_c_re[l], ssm_c_im[l])
        y_s5 = _s5(u, mats, ssm_d[l], bsz)
        z_bound = (SB_HEAD_DIM * SB_Q_SCALE * SB_BOUND_SLACK
                   * jnp.max(jnp.abs(sb_g_q[l].astype(F32))) * jnp.max(jnp.abs(sb_g_k[l].astype(F32))))
        y_sb = _sb_attn(q.reshape(bsz, seq, sb_w), k.reshape(bsz, seq, sb_w), v.reshape(bsz, seq, sb_w), z_bound)
        mk, mv = _mem_kv(mem.astype(F32), g_mem[l], xa_w_kv[l], xa_g_k[l])
        x = _tail(x, y_s5.reshape(bsz, seq, ssm_w), y_sb, ssm_w_glu[l], g_out_ssm[l], g_out_sb[l], w_out[l],
                  g_xa[l], xa_w_q[l], xa_g_q[l], mk, mv, xa_w_o[l], g_mlp[l], w_up[l], w_down[l])
    return x
```

```python
import functools

import jax
import numpy as np
import jax.numpy as jnp
from jax import lax
from jax.experimental import pallas as pl
from jax.experimental.pallas import tpu as pltpu

F32 = jnp.float32
BF16 = jnp.bfloat16

NORM_EPS = 1e-6
SSM_GROUP = 16
SSM_STATE = 64
SSM_CHUNK = 16
S5_COL_GROUPS = 8
SB_HEAD_DIM = 64
XA_HEADS = 4
XA_HEAD_DIM = 128
LANES = 128
LOG2E = 1.4426950408889634
SB_Q_SCALE = SB_HEAD_DIM ** -0.5 * LOG2E
SB_BOUND_SLACK = 1.05
VMEM_LIMIT_BYTES = 56 * 1024 * 1024


def _params(*sem):
    return pltpu.CompilerParams(dimension_semantics=sem, vmem_limit_bytes=VMEM_LIMIT_BYTES)


def _const_spec(shape):
    zeros = (0,) * len(shape)
    return pl.BlockSpec(shape, lambda *_: zeros, pipeline_mode=pl.Buffered(1))


def _rms(x, g):
    return x * lax.rsqrt(jnp.mean(x * x, axis=-1, keepdims=True) + NORM_EPS) * g


def _token_tile(n, want):
    t = min(n, want)
    assert n % t == 0
    return t


def _in_proj_kernel(x_ref, g_ref, w_ref, seg_ref, gq_ref, gk_ref,
                    u_ref, q_ref, k_ref, v_ref, *, ssm_w, sb_w):
    h = _rms(x_ref[...], g_ref[...])
    proj = jnp.dot(h.astype(BF16), w_ref[...], preferred_element_type=F32)
    u_ref[...] = proj[:, :ssm_w]
    q = proj[:, ssm_w:ssm_w + sb_w]
    k = proj[:, ssm_w + sb_w:ssm_w + 2 * sb_w]
    v_ref[...] = proj[:, ssm_w + 2 * sb_w:].astype(BF16)
    seg = seg_ref[...]

    def head_norm(t, g):
        ms = jnp.dot((t * t).astype(BF16), seg, preferred_element_type=F32)
        return t * lax.rsqrt(ms + NORM_EPS) * g

    q_ref[...] = head_norm(q, gq_ref[...]).astype(BF16)
    k_ref[...] = head_norm(k, gk_ref[...]).astype(BF16)


def _in_proj(x2, g_mix, w_in, g_q, g_k, ssm_w, sb_w):
    n, d = x2.shape
    tm = _token_tile(n, 1024)
    heads = sb_w // SB_HEAD_DIM
    lane_head = np.arange(sb_w) // SB_HEAD_DIM
    seg = jnp.asarray((lane_head[:, None] == lane_head[None, :]) / SB_HEAD_DIM, BF16)
    gq = (jnp.tile(g_q.astype(F32), heads) * SB_Q_SCALE)[None]
    gk = jnp.tile(g_k.astype(F32), heads)[None]
    kern = functools.partial(_in_proj_kernel, ssm_w=ssm_w, sb_w=sb_w)
    tok = lambda w: pl.BlockSpec((tm, w), lambda i: (i, 0))
    return pl.pallas_call(
        kern,
        grid=(n // tm,),
        in_specs=[tok(d), _const_spec((1, d)), _const_spec(w_in.shape), _const_spec((sb_w, sb_w)),
                  _const_spec((1, sb_w)), _const_spec((1, sb_w))],
        out_specs=[tok(ssm_w), tok(sb_w), tok(sb_w), tok(sb_w)],
        out_shape=[jax.ShapeDtypeStruct((n, ssm_w), F32)] + [jax.ShapeDtypeStruct((n, sb_w), BF16)] * 3,
        compiler_params=_params("parallel"),
        name="in_proj",
    )(x2, g_mix.astype(F32)[None], w_in.astype(BF16), seg, gq, gk)


def _s5_mats(a_re, a_im, log_dt, b_re, b_im, c_re, c_im):
    hp = lax.Precision.HIGHEST
    L = SSM_CHUNK
    g, p = a_re.shape
    nc = b_re.shape[-1]
    gt = LANES // nc
    nt = g // gt
    lr, li = a_re.astype(F32), a_im.astype(F32)
    dt = jnp.exp(log_dt.astype(F32))[:, None]
    n = jnp.arange(L + 1, dtype=F32)[:, None, None]
    mag = jnp.exp(n * (lr * dt)[None])
    pr, pi = mag * jnp.cos(n * (li * dt)[None]), mag * jnp.sin(n * (li * dt)[None])
    xr, xi = pr[1] - 1.0, pi[1]
    den = lr * lr + li * li
    fr, fi = (xr * lr + xi * li) / den, (xi * lr - xr * li) / den
    brt = jnp.swapaxes(b_re.astype(F32), 1, 2)
    bit = jnp.swapaxes(b_im.astype(F32), 1, 2)
    bbr = fr[:, None, :] * brt - fi[:, None, :] * bit
    bbi = fr[:, None, :] * bit + fi[:, None, :] * brt
    cr, ci = c_re.astype(F32), c_im.astype(F32)
    mr = pr[:, :, None, :] * bbr[None] - pi[:, :, None, :] * bbi[None]
    mi = pr[:, :, None, :] * bbi[None] + pi[:, :, None, :] * bbr[None]
    taps = (jnp.einsum('ngdp,gcp->ngdc', mr[:L], cr, precision=hp)
            - jnp.einsum('ngdp,gcp->ngdc', mi[:L], ci, precision=hp))
    tile_rows = lambda a: a.reshape((L, nt, gt) + a.shape[2:])
    kt = tile_rows(taps).transpose(1, 0, 2, 3, 4).reshape(nt, L * LANES, nc)
    rev = L - 1 - jnp.arange(L)
    pcat = jnp.concatenate([mr[rev], mi[rev]], axis=-1)
    pc = tile_rows(pcat).transpose(1, 0, 2, 3, 4).reshape(nt, L * LANES, 2 * p)
    qr = cr[None] * pr[1:, :, None, :] - ci[None] * pi[1:, :, None, :]
    qi = cr[None] * pi[1:, :, None, :] + ci[None] * pr[1:, :, None, :]
    qcat = jnp.concatenate([qr, -qi], axis=-1)
    qct = tile_rows(qcat).transpose(1, 0, 3, 2, 4).reshape(nt, L * nc, gt * 2 * p)
    coef = jnp.stack([jnp.concatenate([pr[L], pr[L]], -1),
                      jnp.concatenate([-pi[L], pi[L]], -1)], axis=1)
    cf = coef.reshape(nt, gt, 2, 2 * p).transpose(0, 2, 1, 3).reshape(nt, 2, 2 * p * gt)
    return kt.astype(BF16), pc.astype(BF16), qct.astype(BF16), cf.astype(F32)


def _spread_consts(nc):
    L = SSM_CHUNK
    lane = np.arange(LANES)
    to_tile = (np.arange(nc)[:, None] == (lane % nc)[None, :])
    col = np.arange(L * LANES)
    to_row = (np.arange(L * nc)[:, None] == (col // LANES * nc + col % nc)[None, :])
    return jnp.asarray(to_tile, BF16), jnp.asarray(to_row, BF16)


def _s5_kernel(u_ref, kt_ref, pc_ref, qct_ref, to_tile_ref, to_row_ref, coef_ref, d_ref, y_ref,
               tt_scr, pp_scr, qq_scr, e_scr, s_scr, *, sw, nc):
    L = SSM_CHUNK
    gt = LANES // nc
    lw = L * LANES

    @pl.when(pl.program_id(1) == 0)
    def _expand_tile_matrices():
        row_g = lax.broadcasted_iota(jnp.int32, (L * LANES, LANES), 0) // nc % gt
        col_g = lax.broadcasted_iota(jnp.int32, (L * LANES, LANES), 1) // nc
        taps = jnp.dot(kt_ref[0], to_tile_ref[...], preferred_element_type=F32)
        taps = jnp.where(row_g == col_g, taps, 0.0).astype(BF16)
        zero = jnp.zeros((LANES, LANES), BF16)
        for j in range(L):
            for i in range(L):
                lag = (i - j) * LANES
                tt_scr[j * LANES:(j + 1) * LANES, i * LANES:(i + 1) * LANES] = (
                    taps[lag:lag + LANES, :] if i >= j else zero)
        pc = pc_ref[0].astype(F32)
        for g in range(gt):
            pp_scr[:, g * LANES:(g + 1) * LANES] = jnp.where(row_g == g, pc, 0.0).astype(BF16)
        full = lax.dot_general(qct_ref[0], to_row_ref[...], (((0,), (0,)), ((), ())), preferred_element_type=F32)
        rg = lax.broadcasted_iota(jnp.int32, full.shape, 0) // LANES
        cg = lax.broadcasted_iota(jnp.int32, full.shape, 1) // nc % gt
        qq_scr[...] = jnp.where(rg == cg, full, 0.0).astype(BF16)

    nk = u_ref.shape[0]
    u_steps = jnp.swapaxes(u_ref[...], 0, 1)
    u = jnp.concatenate([u_steps[t] for t in range(L)], axis=1).astype(BF16)
    e_s = jnp.dot(u, pp_scr[...], preferred_element_type=F32)
    e_scr[:, :sw] = e_s
    for g in range(gt):
        e_scr[:, sw + g * LANES:sw + (g + 1) * LANES] = pltpu.roll(
            e_s[:, g * LANES:(g + 1) * LANES], LANES // 2, axis=1)
    c1 = coef_ref[0, 0:1, :]
    c2 = coef_ref[0, 1:2, :]

    def step(k, carry):
        s, w = carry
        s_scr[pl.ds(k, 1), :] = s
        e = e_scr[pl.ds(k, 1), :]
        return s * c1 + w * c2 + e[:, :sw], w * c1 - s * c2 + e[:, sw:]

    zero = jnp.zeros((1, sw), F32)
    lax.fori_loop(0, nk, step, (zero, zero), unroll=True)
    cw = lw // S5_COL_GROUPS
    y = jnp.concatenate(
        [jnp.dot(u[:, :(n + 1) * cw], tt_scr[:(n + 1) * cw, n * cw:(n + 1) * cw], preferred_element_type=F32)
         for n in range(S5_COL_GROUPS)], axis=1)
    y += jnp.dot(s_scr[...].astype(BF16), qq_scr[...], preferred_element_type=F32)
    y_steps = jnp.stack([y[:, i * LANES:(i + 1) * LANES] for i in range(L)], axis=0)
    y_ref[...] = jnp.swapaxes(y_steps, 0, 1) + d_ref[...] * u_ref[...]


def _s5(u, mats, d_skip, bsz):
    kt, pc, qct, cf = mats
    n, width = u.shape
    L = SSM_CHUNK
    nt = width // LANES
    nk = n // L // bsz
    lw, sw = L * LANES, cf.shape[-1]
    nc = kt.shape[-1]
    to_tile, to_row = _spread_consts(nc)
    kern = functools.partial(_s5_kernel, sw=sw, nc=nc)
    per_t = lambda a, b: pl.BlockSpec((1, a, b), lambda t, s: (t, 0, 0))
    seq_blk = pl.BlockSpec((nk, L, LANES), lambda t, s: (s, 0, t))
    y = pl.pallas_call(
        kern,
        grid=(nt, bsz),
        in_specs=[seq_blk, per_t(lw, nc), per_t(lw, LANES), per_t(L * nc, sw),
                  _const_spec(to_tile.shape), _const_spec(to_row.shape), per_t(2, sw), per_t(1, LANES)],
        out_specs=seq_blk,
        out_shape=jax.ShapeDtypeStruct((n // L, L, width), F32),
        scratch_shapes=[pltpu.VMEM((lw, lw), BF16), pltpu.VMEM((lw, sw), BF16), pltpu.VMEM((sw, lw), BF16),
                        pltpu.VMEM((nk, 2 * sw), F32), pltpu.VMEM((nk, sw), F32)],
        compiler_params=_params("parallel", "arbitrary"),
        name="s5_scan",
    )(u.reshape(n // L, L, width), kt, pc, qct, to_tile, to_row, cf, d_skip.astype(F32).reshape(nt, 1, LANES))
    return y.reshape(n, width)


F32_ZERO_EXP2 = 160.0
SB_WAVE = 2


def _sb_attn_kernel(thr_ref, q_ref, k_ref, v_ref, tri_ref, o_ref, *, blk, nlb):
    i = pl.program_id(1)
    lane = lax.broadcasted_iota(jnp.int32, (blk, LANES), 1)
    first = lane < SB_HEAD_DIM
    qs = []
    for c in range(nlb):
        q = q_ref[0, :, c * LANES:(c + 1) * LANES]
        zq = jnp.zeros_like(q)
        qs.append(jnp.concatenate([jnp.where(first, q, zq), jnp.where(first, zq, q)], axis=0))
    tri = tri_ref[...]
    row = lax.broadcasted_iota(jnp.int32, (2 * blk, blk), 0)
    col = lax.broadcasted_iota(jnp.int32, (2 * blk, blk), 1)
    below = col < jnp.where(row >= blk, row - blk, row)

    def block(j, carry, acc, mask):
        start = pl.multiple_of(j * blk, blk)
        new_acc, new_carry = list(acc), list(carry)
        for c0 in range(0, nlb, SB_WAVE):
            wave = range(c0, min(nlb, c0 + SB_WAVE))
            zs, ps = {}, {}
            for c in wave:
                kb = k_ref[0, pl.ds(start, blk), c * LANES:(c + 1) * LANES]
                zs[c] = lax.dot_general(qs[c], kb, (((1,), (1,)), ((), ())), preferred_element_type=F32)
            for c in wave:
                z = zs[c]
                sp = jnp.maximum(z, 0.0) + jnp.log(1.0 + jnp.exp2(-jnp.abs(z))) * LOG2E
                if mask is not None:
                    sp = jnp.where(mask, sp, 0.0)
                    zs[c] = jnp.where(mask, z, -1e30)
                ps[c] = sp.astype(BF16)
            cums = {c: jnp.dot(ps[c], tri, preferred_element_type=F32) for c in wave}
            ws = {c: jnp.exp2(zs[c] - cums[c] - carry[c]).astype(BF16) for c in wave}
            for c in wave:
                vb = v_ref[0, pl.ds(start, blk), c * LANES:(c + 1) * LANES]
                new_acc[c] = acc[c] + jnp.dot(ws[c], vb, preferred_element_type=F32)
                new_carry[c] = carry[c] + cums[c][:, 0:1]
        return tuple(new_carry), tuple(new_acc)

    zc = tuple(jnp.zeros((2 * blk, 1), F32) for _ in range(nlb))
    za = tuple(jnp.zeros((2 * blk, LANES), F32) for _ in range(nlb))
    carry, acc = block(i, zc, za, below)
    no_prev = jnp.where(i == 0, jnp.float32(1e30), jnp.float32(0.0))
    carry, acc = block(jnp.maximum(i - 1, 0), tuple(cv + no_prev for cv in carry), acc, None)
    thr = thr_ref[0]

    def row_min(carry):
        return jnp.min(functools.reduce(jnp.minimum, carry))

    def more(st):
        return jnp.logical_and(st[0] >= 0, st[1] <= thr)

    def step(st):
        j, _, carry, acc = st
        carry, acc = block(j, carry, acc, None)
        return j - 1, row_min(carry), carry, acc

    _, _, carry, acc = lax.while_loop(more, step, (i - 2, row_min(carry), carry, acc))
    for c in range(nlb):
        o_ref[0, :, c * LANES:(c + 1) * LANES] = jnp.where(first, acc[c][:blk], acc[c][blk:]).astype(o_ref.dtype)


def _sb_attn(q, k, v, z_bound):
    bsz, seq, width = q.shape
    blk = _token_tile(seq, 256)
    assert width % LANES == 0
    tri = jnp.asarray(np.arange(blk)[:, None] >= np.arange(blk)[None, :], BF16)
    thr = (z_bound + F32_ZERO_EXP2).astype(F32).reshape(1)
    kern = functools.partial(_sb_attn_kernel, blk=blk, nlb=width // LANES)
    q_blk = pl.BlockSpec((1, blk, width), lambda b, i: (b, i, 0))
    whole_seq = pl.BlockSpec((1, seq, width), lambda b, i: (b, 0, 0), pipeline_mode=pl.Buffered(1))
    return pl.pallas_call(
        kern,
        grid=(bsz, seq // blk),
        in_specs=[pl.BlockSpec(memory_space=pltpu.SMEM), q_blk, whole_seq, whole_seq,
                  _const_spec((blk, blk))],
        out_specs=q_blk,
        out_shape=jax.ShapeDtypeStruct((bsz, seq, width), BF16),
        compiler_params=_params("parallel", "arbitrary"),
        name="sb_attn",
    )(thr, q, k, v, tri)


def _mix_out_body(x, ys, sb, wg_ref, gs_ref, gb_ref, wo_ref):
    ssm_w = ys.shape[1]
    y = jax.nn.gelu(ys)
    gate = jnp.dot(y.astype(BF16), wg_ref[...], preferred_element_type=F32)
    y = y * jax.nn.sigmoid(gate)
    yn = _rms(y, gs_ref[...]).astype(BF16)
    sn = _rms(sb.astype(F32), gb_ref[...]).astype(BF16)
    out = jnp.dot(yn, wo_ref[:ssm_w, :], preferred_element_type=F32)
    out += jnp.dot(sn, wo_ref[ssm_w:, :], preferred_element_type=F32)
    return x + out


def _mem_kv_kernel(m_ref, g_ref, w_ref, gk_ref, k_ref, v_ref, *, xa_w):
    h = _rms(m_ref[0], g_ref[...])
    kv = jnp.dot(h.astype(BF16), w_ref[...], preferred_element_type=F32)
    for hd in range(xa_w // XA_HEAD_DIM):
        sl = slice(hd * XA_HEAD_DIM, (hd + 1) * XA_HEAD_DIM)
        k_ref[0, :, sl] = _rms(kv[:, sl], gk_ref[...]).astype(BF16)
    v_ref[0] = kv[:, xa_w:].astype(BF16)


def _mem_kv(mem, g_mem, w_kv, g_k):
    bsz, mlen, d = mem.shape
    xa_w = w_kv.shape[1] // 2
    kern = functools.partial(_mem_kv_kernel, xa_w=xa_w)
    per_b = lambda w: pl.BlockSpec((1, mlen, w), lambda b: (b, 0, 0))
    return pl.pallas_call(
        kern,
        grid=(bsz,),
        in_specs=[per_b(d), _const_spec((1, d)), _const_spec(w_kv.shape), _const_spec((1, XA_HEAD_DIM))],
        out_specs=[per_b(xa_w), per_b(xa_w)],
        out_shape=[jax.ShapeDtypeStruct((bsz, mlen, xa_w), BF16)] * 2,
        compiler_params=_params("parallel"),
        name="mem_kv",
    )(mem, g_mem.astype(F32)[None], w_kv.astype(BF16), g_k.astype(F32)[None])


def _xattn_body(x, g_ref, wq_ref, gq_ref, k_ref, v_ref, wo_ref):
    h = _rms(x, g_ref[...])
    q = jnp.dot(h.astype(BF16), wq_ref[...], preferred_element_type=F32)
    heads = []
    for hd in range(q.shape[1] // XA_HEAD_DIM):
        sl = slice(hd * XA_HEAD_DIM, (hd + 1) * XA_HEAD_DIM)
        qh = _rms(q[:, sl], gq_ref[...]).astype(BF16)
        s = lax.dot_general(qh, k_ref[0, :, sl], (((1,), (1,)), ((), ())), preferred_element_type=F32)
        e = jnp.exp(s - jnp.max(s, axis=-1, keepdims=True))
        pv = jnp.dot(e.astype(BF16), v_ref[0, :, sl], preferred_element_type=F32)
        heads.append((pv / jnp.sum(e, axis=-1, keepdims=True)).astype(BF16))
    o = jnp.concatenate(heads, axis=-1)
    return x + jnp.dot(o, wo_ref[...], preferred_element_type=F32)


MLP_FF_CHUNK = 1024


def _mlp_body(x, g_ref, wu_ref, wd_ref):
    h = _rms(x, g_ref[...]).astype(BF16)
    ff = wu_ref.shape[1]
    ff_chunk = min(ff, MLP_FF_CHUNK)
    acc = x
    for c in range(ff // ff_chunk):
        sl = slice(c * ff_chunk, (c + 1) * ff_chunk)
        a = jnp.maximum(jnp.dot(h, wu_ref[:, sl], preferred_element_type=F32), 0.0)
        acc = acc + jnp.dot((a * a).astype(BF16), wd_ref[sl, :], preferred_element_type=F32)
    return acc


def _tail_kernel(x_ref, ys_ref, sb_ref, wg_ref, gs_ref, gb_ref, wo_ref,
                 gxa_ref, wq_ref, gq_ref, k_ref, v_ref, wxo_ref, gm_ref, wu_ref, wd_ref, o_ref):
    x = _mix_out_body(x_ref[0], ys_ref[0], sb_ref[0], wg_ref, gs_ref, gb_ref, wo_ref)
    x = _xattn_body(x, gxa_ref, wq_ref, gq_ref, k_ref, v_ref, wxo_ref)
    o_ref[0] = _mlp_body(x, gm_ref, wu_ref, wd_ref)


def _tail(x, y_s5, y_sb, w_glu, g_ssm, g_sb, w_out, g_xa, w_q, g_q, mk, mv, w_o, g_mlp, w_up, w_down):
    bsz, seq, d = x.shape
    ssm_w, sb_w = y_s5.shape[-1], y_sb.shape[-1]
    mlen, xa_w = mk.shape[1], mk.shape[2]
    tm = _token_tile(seq, 1024)
    gq = (g_q.astype(F32) * (XA_HEAD_DIM ** -0.5))[None]
    tok = lambda w: pl.BlockSpec((1, tm, w), lambda b, i: (b, i, 0))
    per_b = pl.BlockSpec((1, mlen, xa_w), lambda b, i: (b, 0, 0))
    vec = lambda g: g.astype(F32)[None]
    return pl.pallas_call(
        _tail_kernel,
        grid=(bsz, seq // tm),
        in_specs=[tok(d), tok(ssm_w), tok(sb_w),
                  _const_spec(w_glu.shape), _const_spec((1, ssm_w)), _const_spec((1, sb_w)), _const_spec(w_out.shape),
                  _const_spec((1, d)), _const_spec(w_q.shape), _const_spec((1, XA_HEAD_DIM)), per_b, per_b,
                  _const_spec(w_o.shape),
                  _const_spec((1, d)), _const_spec(w_up.shape), _const_spec(w_down.shape)],
        out_specs=tok(d),
        out_shape=jax.ShapeDtypeStruct((bsz, seq, d), F32),
        compiler_params=_params("parallel", "parallel"),
        name="tail",
    )(x, y_s5, y_sb, w_glu.astype(BF16), vec(g_ssm), vec(g_sb), w_out.astype(BF16),
      vec(g_xa), w_q.astype(BF16), gq, mk, mv, w_o.astype(BF16),
      vec(g_mlp), w_up.astype(BF16), w_down.astype(BF16))


def kernel(x, mem, g_mix, w_in, ssm_a_re, ssm_a_im, ssm_log_dt, ssm_b_re, ssm_b_im, ssm_c_re, ssm_c_im,
           ssm_d, ssm_w_glu, sb_g_q, sb_g_k, g_out_ssm, g_out_sb, w_out, g_xa, g_mem, xa_w_q, xa_w_kv,
           xa_g_q, xa_g_k, xa_w_o, g_mlp, w_up, w_down):
    bsz, seq, d = x.shape
    n = bsz * seq
    ssm_w = ssm_d.shape[-1]
    sb_w = g_out_sb.shape[-1]
    x = x.astype(F32)
    for l in range(g_mix.shape[0]):
        u, q, k, v = _in_proj(x.reshape(n, d), g_mix[l], w_in[l], sb_g_q[l], sb_g_k[l], ssm_w, sb_w)
        mats = _s5_mats(ssm_a_re[l], ssm_a_im[l], ssm_log_dt[l], ssm_b_re[l], ssm_b_im[l],
                        ssm_c_re[l], ssm_c_im[l])
        y_s5 = _s5(u, mats, ssm_d[l], bsz)
        z_bound = (SB_HEAD_DIM * SB_Q_SCALE * SB_BOUND_SLACK
                   * jnp.max(jnp.abs(sb_g_q[l].astype(F32))) * jnp.max(jnp.abs(sb_g_k[l].astype(F32))))
        y_sb = _sb_attn(q.reshape(bsz, seq, sb_w), k.reshape(bsz, seq, sb_w), v.reshape(bsz, seq, sb_w), z_bound)
        mk, mv = _mem_kv(mem.astype(F32), g_mem[l], xa_w_kv[l], xa_g_k[l])
        x = _tail(x, y_s5.reshape(bsz, seq, ssm_w), y_sb, ssm_w_glu[l], g_out_ssm[l], g_out_sb[l], w_out[l],
                  g_xa[l], xa_w_q[l], xa_g_q[l], mk, mv, xa_w_o[l], g_mlp[l], w_up[l], w_down[l])
    return x
```

```python
import functools

import jax
import numpy as np
import jax.numpy as jnp
from jax import lax
from jax.experimental import pallas as pl
from jax.experimental.pallas import tpu as pltpu

F32 = jnp.float32
BF16 = jnp.bfloat16

NORM_EPS = 1e-6
SSM_GROUP = 16
SSM_STATE = 64
SSM_CHUNK = 16
S5_COL_GROUPS = 8
SB_HEAD_DIM = 64
XA_HEADS = 4
XA_HEAD_DIM = 128
LANES = 128
MXU_DIM = 256
LOG2E = 1.4426950408889634
SB_Q_SCALE = SB_HEAD_DIM ** -0.5 * LOG2E
SB_BOUND_SLACK = 1.05
VMEM_LIMIT_BYTES = 56 * 1024 * 1024


def _params(*sem):
    return pltpu.CompilerParams(dimension_semantics=sem, vmem_limit_bytes=VMEM_LIMIT_BYTES)


def _const_spec(shape):
    zeros = (0,) * len(shape)
    return pl.BlockSpec(shape, lambda *_: zeros, pipeline_mode=pl.Buffered(1))


def _rms(x, g):
    return x * lax.rsqrt(jnp.mean(x * x, axis=-1, keepdims=True) + NORM_EPS) * g


def _token_tile(n, want):
    t = min(n, want)
    assert n % t == 0
    return t


def _in_proj_kernel(x_ref, g_ref, w_ref, seg_ref, gq_ref, gk_ref,
                    u_ref, q_ref, k_ref, v_ref, *, ssm_w, sb_w):
    h = _rms(x_ref[...], g_ref[...])
    proj = jnp.dot(h.astype(BF16), w_ref[...], preferred_element_type=F32)
    u_ref[...] = proj[:, :ssm_w]
    q = proj[:, ssm_w:ssm_w + sb_w]
    k = proj[:, ssm_w + sb_w:ssm_w + 2 * sb_w]
    v_ref[...] = proj[:, ssm_w + 2 * sb_w:].astype(BF16)
    seg = seg_ref[...]

    def head_norm(t, g):
        sq = (t * t).astype(BF16)
        ms = jnp.concatenate([jnp.dot(sq[:, n * MXU_DIM:(n + 1) * MXU_DIM], seg, preferred_element_type=F32)
                              for n in range(t.shape[1] // MXU_DIM)], axis=1)
        return t * lax.rsqrt(ms + NORM_EPS) * g

    q_ref[...] = head_norm(q, gq_ref[...]).astype(BF16)
    k_ref[...] = head_norm(k, gk_ref[...]).astype(BF16)


def _in_proj(x2, g_mix, w_in, g_q, g_k, ssm_w, sb_w):
    n, d = x2.shape
    tm = _token_tile(n, 1024)
    heads = sb_w // SB_HEAD_DIM
    assert MXU_DIM % SB_HEAD_DIM == 0 and sb_w % MXU_DIM == 0
    lane_head = np.arange(MXU_DIM) // SB_HEAD_DIM
    seg = jnp.asarray((lane_head[:, None] == lane_head[None, :]) / SB_HEAD_DIM, BF16)
    gq = (jnp.tile(g_q.astype(F32), heads) * SB_Q_SCALE)[None]
    gk = jnp.tile(g_k.astype(F32), heads)[None]
    kern = functools.partial(_in_proj_kernel, ssm_w=ssm_w, sb_w=sb_w)
    tok = lambda w: pl.BlockSpec((tm, w), lambda i: (i, 0))
    return pl.pallas_call(
        kern,
        grid=(n // tm,),
        in_specs=[tok(d), _const_spec((1, d)), _const_spec(w_in.shape), _const_spec((MXU_DIM, MXU_DIM)),
                  _const_spec((1, sb_w)), _const_spec((1, sb_w))],
        out_specs=[tok(ssm_w), tok(sb_w), tok(sb_w), tok(sb_w)],
        out_shape=[jax.ShapeDtypeStruct((n, ssm_w), F32)] + [jax.ShapeDtypeStruct((n, sb_w), BF16)] * 3,
        compiler_params=_params("parallel"),
        name="in_proj",
    )(x2, g_mix.astype(F32)[None], w_in.astype(BF16), seg, gq, gk)


def _s5_mats(a_re, a_im, log_dt, b_re, b_im, c_re, c_im):
    hp = lax.Precision.HIGHEST
    L = SSM_CHUNK
    g, p = a_re.shape
    nc = b_re.shape[-1]
    gt = LANES // nc
    nt = g // gt
    lr, li = a_re.astype(F32), a_im.astype(F32)
    dt = jnp.exp(log_dt.astype(F32))[:, None]
    n = jnp.arange(L + 1, dtype=F32)[:, None, None]
    mag = jnp.exp(n * (lr * dt)[None])
    pr, pi = mag * jnp.cos(n * (li * dt)[None]), mag * jnp.sin(n * (li * dt)[None])
    xr, xi = pr[1] - 1.0, pi[1]
    den = lr * lr + li * li
    fr, fi = (xr * lr + xi * li) / den, (xi * lr - xr * li) / den
    brt = jnp.swapaxes(b_re.astype(F32), 1, 2)
    bit = jnp.swapaxes(b_im.astype(F32), 1, 2)
    bbr = fr[:, None, :] * brt - fi[:, None, :] * bit
    bbi = fr[:, None, :] * bit + fi[:, None, :] * brt
    cr, ci = c_re.astype(F32), c_im.astype(F32)
    mr = pr[:, :, None, :] * bbr[None] - pi[:, :, None, :] * bbi[None]
    mi = pr[:, :, None, :] * bbi[None] + pi[:, :, None, :] * bbr[None]
    taps = (jnp.einsum('ngdp,gcp->ngdc', mr[:L], cr, precision=hp)
            - jnp.einsum('ngdp,gcp->ngdc', mi[:L], ci, precision=hp))
    tile_rows = lambda a: a.reshape((L, nt, gt) + a.shape[2:])
    kt = tile_rows(taps).transpose(1, 0, 2, 3, 4).reshape(nt, L * LANES, nc)
    rev = L - 1 - jnp.arange(L)
    pcat = jnp.concatenate([mr[rev], mi[rev]], axis=-1)
    pc = tile_rows(pcat).transpose(1, 0, 2, 3, 4).reshape(nt, L * LANES, 2 * p)
    qr = cr[None] * pr[1:, :, None, :] - ci[None] * pi[1:, :, None, :]
    qi = cr[None] * pi[1:, :, None, :] + ci[None] * pr[1:, :, None, :]
    qcat = jnp.concatenate([qr, -qi], axis=-1)
    qct = tile_rows(qcat).transpose(1, 0, 3, 2, 4).reshape(nt, L * nc, gt * 2 * p)
    coef = jnp.stack([jnp.concatenate([pr[L], pr[L]], -1),
                      jnp.concatenate([-pi[L], pi[L]], -1)], axis=1)
    cf = coef.reshape(nt, gt, 2, 2 * p).transpose(0, 2, 1, 3).reshape(nt, 2, 2 * p * gt)
    return kt.astype(BF16), pc.astype(BF16), qct.astype(BF16), cf.astype(F32)


def _spread_consts(nc):
    L = SSM_CHUNK
    lane = np.arange(LANES)
    to_tile = (np.arange(nc)[:, None] == (lane % nc)[None, :])
    col = np.arange(L * LANES)
    to_row = (np.arange(L * nc)[:, None] == (col // LANES * nc + col % nc)[None, :])
    return jnp.asarray(to_tile, BF16), jnp.asarray(to_row, BF16)


def _s5_kernel(u_ref, kt_ref, pc_ref, qct_ref, to_tile_ref, to_row_ref, coef_ref, d_ref, y_ref,
               tt_scr, pp_scr, qq_scr, e_scr, s_scr, *, sw, nc):
    L = SSM_CHUNK
    gt = LANES // nc
    lw = L * LANES

    @pl.when(pl.program_id(1) == 0)
    def _expand_tile_matrices():
        row_g = lax.broadcasted_iota(jnp.int32, (L * LANES, LANES), 0) // nc % gt
        col_g = lax.broadcasted_iota(jnp.int32, (L * LANES, LANES), 1) // nc
        taps = jnp.dot(kt_ref[0], to_tile_ref[...], preferred_element_type=F32)
        taps = jnp.where(row_g == col_g, taps, 0.0).astype(BF16)
        zero = jnp.zeros((LANES, LANES), BF16)
        for j in range(L):
            for i in range(L):
                lag = (i - j) * LANES
                tt_scr[j * LANES:(j + 1) * LANES, i * LANES:(i + 1) * LANES] = (
                    taps[lag:lag + LANES, :] if i >= j else zero)
        pc = pc_ref[0].astype(F32)
        for g in range(gt):
            pp_scr[:, g * LANES:(g + 1) * LANES] = jnp.where(row_g == g, pc, 0.0).astype(BF16)
        full = lax.dot_general(qct_ref[0], to_row_ref[...], (((0,), (0,)), ((), ())), preferred_element_type=F32)
        rg = lax.broadcasted_iota(jnp.int32, full.shape, 0) // LANES
        cg = lax.broadcasted_iota(jnp.int32, full.shape, 1) // nc % gt
        qq_scr[...] = jnp.where(rg == cg, full, 0.0).astype(BF16)

    nk = u_ref.shape[0]
    u_steps = jnp.swapaxes(u_ref[...], 0, 1)
    u = jnp.concatenate([u_steps[t] for t in range(L)], axis=1).astype(BF16)
    e_s = jnp.dot(u, pp_scr[...], preferred_element_type=F32)
    e_scr[:, :sw] = e_s
    for g in range(gt):
        e_scr[:, sw + g * LANES:sw + (g + 1) * LANES] = pltpu.roll(
            e_s[:, g * LANES:(g + 1) * LANES], LANES // 2, axis=1)
    c1 = coef_ref[0, 0:1, :]
    c2 = coef_ref[0, 1:2, :]

    def step(k, carry):
        s, w = carry
        s_scr[pl.ds(k, 1), :] = s
        e = e_scr[pl.ds(k, 1), :]
        return s * c1 + w * c2 + e[:, :sw], w * c1 - s * c2 + e[:, sw:]

    zero = jnp.zeros((1, sw), F32)
    lax.fori_loop(0, nk, step, (zero, zero), unroll=True)
    cw = lw // S5_COL_GROUPS
    y = jnp.concatenate(
        [jnp.dot(u[:, :(n + 1) * cw], tt_scr[:(n + 1) * cw, n * cw:(n + 1) * cw], preferred_element_type=F32)
         for n in range(S5_COL_GROUPS)], axis=1)
    y += jnp.dot(s_scr[...].astype(BF16), qq_scr[...], preferred_element_type=F32)
    y_steps = jnp.stack([y[:, i * LANES:(i + 1) * LANES] for i in range(L)], axis=0)
    y_ref[...] = jnp.swapaxes(y_steps, 0, 1) + d_ref[...] * u_ref[...]


def _s5(u, mats, d_skip, bsz):
    kt, pc, qct, cf = mats
    n, width = u.shape
    L = SSM_CHUNK
    nt = width // LANES
    nk = n // L // bsz
    lw, sw = L * LANES, cf.shape[-1]
    nc = kt.shape[-1]
    to_tile, to_row = _spread_consts(nc)
    kern = functools.partial(_s5_kernel, sw=sw, nc=nc)
    per_t = lambda a, b: pl.BlockSpec((1, a, b), lambda t, s: (t, 0, 0))
    seq_blk = pl.BlockSpec((nk, L, LANES), lambda t, s: (s, 0, t))
    y = pl.pallas_call(
        kern,
        grid=(nt, bsz),
        in_specs=[seq_blk, per_t(lw, nc), per_t(lw, LANES), per_t(L * nc, sw),
                  _const_spec(to_tile.shape), _const_spec(to_row.shape), per_t(2, sw), per_t(1, LANES)],
        out_specs=seq_blk,
        out_shape=jax.ShapeDtypeStruct((n // L, L, width), F32),
        scratch_shapes=[pltpu.VMEM((lw, lw), BF16), pltpu.VMEM((lw, sw), BF16), pltpu.VMEM((sw, lw), BF16),
                        pltpu.VMEM((nk, 2 * sw), F32), pltpu.VMEM((nk, sw), F32)],
        compiler_params=_params("parallel", "arbitrary"),
        name="s5_scan",
    )(u.reshape(n // L, L, width), kt, pc, qct, to_tile, to_row, cf, d_skip.astype(F32).reshape(nt, 1, LANES))
    return y.reshape(n, width)


F32_ZERO_EXP2 = 160.0
SB_WAVE = 2


def _sb_attn_kernel(thr_ref, q_ref, k_ref, v_ref, tri_ref, o_ref, *, blk, nlb):
    i = pl.program_id(1)
    lane = lax.broadcasted_iota(jnp.int32, (blk, LANES), 1)
    first = lane < SB_HEAD_DIM
    qs = []
    for c in range(nlb):
        q = q_ref[0, :, c * LANES:(c + 1) * LANES]
        zq = jnp.zeros_like(q)
        qs.append(jnp.concatenate([jnp.where(first, q, zq), jnp.where(first, zq, q)], axis=0))
    tri = tri_ref[...]
    row = lax.broadcasted_iota(jnp.int32, (2 * blk, blk), 0)
    col = lax.broadcasted_iota(jnp.int32, (2 * blk, blk), 1)
    below = col < jnp.where(row >= blk, row - blk, row)

    def block(j, carry, acc, mask):
        start = pl.multiple_of(j * blk, blk)
        new_acc, new_carry = list(acc), list(carry)
        for c0 in range(0, nlb, SB_WAVE):
            wave = range(c0, min(nlb, c0 + SB_WAVE))
            zs, ps = {}, {}
            for c in wave:
                kb = k_ref[0, pl.ds(start, blk), c * LANES:(c + 1) * LANES]
                zs[c] = lax.dot_general(qs[c], kb, (((1,), (1,)), ((), ())), preferred_element_type=F32)
            for c in wave:
                z = zs[c]
                sp = jnp.maximum(z, 0.0) + jnp.log(1.0 + jnp.exp2(-jnp.abs(z))) * LOG2E
                if mask is not None:
                    sp = jnp.where(mask, sp, 0.0)
                    zs[c] = jnp.where(mask, z, -1e30)
                ps[c] = sp.astype(BF16)
            cums = {c: jnp.dot(ps[c], tri, preferred_element_type=F32) for c in wave}
            ws = {c: jnp.exp2(zs[c] - cums[c] - carry[c]).astype(BF16) for c in wave}
            for c in wave:
                vb = v_ref[0, pl.ds(start, blk), c * LANES:(c + 1) * LANES]
                new_acc[c] = acc[c] + jnp.dot(ws[c], vb, preferred_element_type=F32)
                new_carry[c] = carry[c] + cums[c][:, 0:1]
        return tuple(new_carry), tuple(new_acc)

    zc = tuple(jnp.zeros((2 * blk, 1), F32) for _ in range(nlb))
    za = tuple(jnp.zeros((2 * blk, LANES), F32) for _ in range(nlb))
    carry, acc = block(i, zc, za, below)
    no_prev = jnp.where(i == 0, jnp.float32(1e30), jnp.float32(0.0))
    carry, acc = block(jnp.maximum(i - 1, 0), tuple(cv + no_prev for cv in carry), acc, None)
    thr = thr_ref[0]

    def row_min(carry):
        return jnp.min(functools.reduce(jnp.minimum, carry))

    def more(st):
        return jnp.logical_and(st[0] >= 0, st[1] <= thr)

    def step(st):
        j, _, carry, acc = st
        carry, acc = block(j, carry, acc, None)
        return j - 1, row_min(carry), carry, acc

    _, _, carry, acc = lax.while_loop(more, step, (i - 2, row_min(carry), carry, acc))
    for c in range(nlb):
        o_ref[0, :, c * LANES:(c + 1) * LANES] = jnp.where(first, acc[c][:blk], acc[c][blk:]).astype(o_ref.dtype)


def _sb_attn(q, k, v, z_bound):
    bsz, seq, width = q.shape
    blk = _token_tile(seq, 256)
    assert width % LANES == 0
    tri = jnp.asarray(np.arange(blk)[:, None] >= np.arange(blk)[None, :], BF16)
    thr = (z_bound + F32_ZERO_EXP2).astype(F32).reshape(1)
    kern = functools.partial(_sb_attn_kernel, blk=blk, nlb=width // LANES)
    q_blk = pl.BlockSpec((1, blk, width), lambda b, i: (b, i, 0))
    whole_seq = pl.BlockSpec((1, seq, width), lambda b, i: (b, 0, 0), pipeline_mode=pl.Buffered(1))
    return pl.pallas_call(
        kern,
        grid=(bsz, seq // blk),
        in_specs=[pl.BlockSpec(memory_space=pltpu.SMEM), q_blk, whole_seq, whole_seq,
                  _const_spec((blk, blk))],
        out_specs=q_blk,
        out_shape=jax.ShapeDtypeStruct((bsz, seq, width), BF16),
        compiler_params=_params("parallel", "arbitrary"),
        name="sb_attn",
    )(thr, q, k, v, tri)


def _mix_out_body(x, ys, sb, wg_ref, gs_ref, gb_ref, wo_ref):
    ssm_w = ys.shape[1]
    y = jax.nn.gelu(ys)
    gate = jnp.dot(y.astype(BF16), wg_ref[...], preferred_element_type=F32)
    y = y * jax.nn.sigmoid(gate)
    yn = _rms(y, gs_ref[...]).astype(BF16)
    sn = _rms(sb.astype(F32), gb_ref[...]).astype(BF16)
    out = jnp.dot(yn, wo_ref[:ssm_w, :], preferred_element_type=F32)
    out += jnp.dot(sn, wo_ref[ssm_w:, :], preferred_element_type=F32)
    return x + out


def _mem_kv_kernel(m_ref, g_ref, w_ref, gk_ref, k_ref, v_ref, *, xa_w):
    h = _rms(m_ref[0], g_ref[...])
    kv = jnp.dot(h.astype(BF16), w_ref[...], preferred_element_type=F32)
    mlen = kv.shape[0]
    for hd in range(xa_w // XA_HEAD_DIM):
        sl = slice(hd * XA_HEAD_DIM, (hd + 1) * XA_HEAD_DIM)
        k_ref[0, :, sl] = _rms(kv[:, sl], gk_ref[...]).astype(BF16)
    v_ref[...] = jnp.zeros_like(v_ref)
    for hd in range(xa_w // XA_HEAD_DIM):
        pair, side = divmod(hd, 2)
        v_ref[0, pair, side * mlen:(side + 1) * mlen, side * XA_HEAD_DIM:(side + 1) * XA_HEAD_DIM] = (
            kv[:, xa_w + hd * XA_HEAD_DIM:xa_w + (hd + 1) * XA_HEAD_DIM].astype(BF16))


def _mem_kv(mem, g_mem, w_kv, g_k):
    bsz, mlen, d = mem.shape
    xa_w = w_kv.shape[1] // 2
    pairs = xa_w // XA_HEAD_DIM // 2
    kern = functools.partial(_mem_kv_kernel, xa_w=xa_w)
    per_b = lambda w: pl.BlockSpec((1, mlen, w), lambda b: (b, 0, 0))
    return pl.pallas_call(
        kern,
        grid=(bsz,),
        in_specs=[per_b(d), _const_spec((1, d)), _const_spec(w_kv.shape), _const_spec((1, XA_HEAD_DIM))],
        out_specs=[per_b(xa_w), pl.BlockSpec((1, pairs, 2 * mlen, 2 * XA_HEAD_DIM), lambda b: (b, 0, 0, 0))],
        out_shape=[jax.ShapeDtypeStruct((bsz, mlen, xa_w), BF16),
                   jax.ShapeDtypeStruct((bsz, pairs, 2 * mlen, 2 * XA_HEAD_DIM), BF16)],
        compiler_params=_params("parallel"),
        name="mem_kv",
    )(mem, g_mem.astype(F32)[None], w_kv.astype(BF16), g_k.astype(F32)[None])


def _xattn_body(x, g_ref, wq_ref, gq_ref, k_ref, v_ref, wo_ref):
    h = _rms(x, g_ref[...])
    q = jnp.dot(h.astype(BF16), wq_ref[...], preferred_element_type=F32)
    es, sums = [], []
    for hd in range(q.shape[1] // XA_HEAD_DIM):
        sl = slice(hd * XA_HEAD_DIM, (hd + 1) * XA_HEAD_DIM)
        qh = _rms(q[:, sl], gq_ref[...]).astype(BF16)
        s = lax.dot_general(qh, k_ref[0, :, sl], (((1,), (1,)), ((), ())), preferred_element_type=F32)
        e = jnp.exp(s - jnp.max(s, axis=-1, keepdims=True))
        es.append(e.astype(BF16))
        sums.append(jnp.sum(e, axis=-1, keepdims=True))
    heads = []
    for pair in range(len(es) // 2):
        a, b = 2 * pair, 2 * pair + 1
        pv = jnp.dot(jnp.concatenate([es[a], es[b]], axis=1), v_ref[0, pair], preferred_element_type=F32)
        heads.append((pv[:, :XA_HEAD_DIM] / sums[a]).astype(BF16))
        heads.append((pv[:, XA_HEAD_DIM:] / sums[b]).astype(BF16))
    o = jnp.concatenate(heads, axis=-1)
    return x + jnp.dot(o, wo_ref[...], preferred_element_type=F32)


MLP_FF_CHUNK = 1024


def _mlp_body(x, g_ref, wu_ref, wd_ref):
    h = _rms(x, g_ref[...]).astype(BF16)
    ff = wu_ref.shape[1]
    ff_chunk = min(ff, MLP_FF_CHUNK)
    acc = x
    for c in range(ff // ff_chunk):
        sl = slice(c * ff_chunk, (c + 1) * ff_chunk)
        a = jnp.maximum(jnp.dot(h, wu_ref[:, sl], preferred_element_type=F32), 0.0)
        acc = acc + jnp.dot((a * a).astype(BF16), wd_ref[sl, :], preferred_element_type=F32)
    return acc


def _tail_kernel(x_ref, ys_ref, sb_ref, wg_ref, gs_ref, gb_ref, wo_ref,
                 gxa_ref, wq_ref, gq_ref, k_ref, v_ref, wxo_ref, gm_ref, wu_ref, wd_ref, o_ref):
    x = _mix_out_body(x_ref[0], ys_ref[0], sb_ref[0], wg_ref, gs_ref, gb_ref, wo_ref)
    x = _xattn_body(x, gxa_ref, wq_ref, gq_ref, k_ref, v_ref, wxo_ref)
    o_ref[0] = _mlp_body(x, gm_ref, wu_ref, wd_ref)


def _tail(x, y_s5, y_sb, w_glu, g_ssm, g_sb, w_out, g_xa, w_q, g_q, mk, mv, w_o, g_mlp, w_up, w_down):
    bsz, seq, d = x.shape
    ssm_w, sb_w = y_s5.shape[-1], y_sb.shape[-1]
    mlen, xa_w = mk.shape[1], mk.shape[2]
    tm = _token_tile(seq, 1024)
    gq = (g_q.astype(F32) * (XA_HEAD_DIM ** -0.5))[None]
    tok = lambda w: pl.BlockSpec((1, tm, w), lambda b, i: (b, i, 0))
    per_b = pl.BlockSpec((1, mlen, xa_w), lambda b, i: (b, 0, 0), pipeline_mode=pl.Buffered(1))
    v_pairs = pl.BlockSpec((1,) + mv.shape[1:], lambda b, i: (b, 0, 0, 0), pipeline_mode=pl.Buffered(1))
    vec = lambda g: g.astype(F32)[None]
    return pl.pallas_call(
        _tail_kernel,
        grid=(bsz, seq // tm),
        in_specs=[tok(d), tok(ssm_w), tok(sb_w),
                  _const_spec(w_glu.shape), _const_spec((1, ssm_w)), _const_spec((1, sb_w)), _const_spec(w_out.shape),
                  _const_spec((1, d)), _const_spec(w_q.shape), _const_spec((1, XA_HEAD_DIM)), per_b, v_pairs,
                  _const_spec(w_o.shape),
                  _const_spec((1, d)), _const_spec(w_up.shape), _const_spec(w_down.shape)],
        out_specs=tok(d),
        out_shape=jax.ShapeDtypeStruct((bsz, seq, d), F32),
        compiler_params=_params("parallel", "parallel"),
        name="tail",
    )(x, y_s5, y_sb, w_glu.astype(BF16), vec(g_ssm), vec(g_sb), w_out.astype(BF16),
      vec(g_xa), w_q.astype(BF16), gq, mk, mv, w_o.astype(BF16),
      vec(g_mlp), w_up.astype(BF16), w_down.astype(BF16))


def kernel(x, mem, g_mix, w_in, ssm_a_re, ssm_a_im, ssm_log_dt, ssm_b_re, ssm_b_im, ssm_c_re, ssm_c_im,
           ssm_d, ssm_w_glu, sb_g_q, sb_g_k, g_out_ssm, g_out_sb, w_out, g_xa, g_mem, xa_w_q, xa_w_kv,
           xa_g_q, xa_g_k, xa_w_o, g_mlp, w_up, w_down):
    bsz, seq, d = x.shape
    n = bsz * seq
    ssm_w = ssm_d.shape[-1]
    sb_w = g_out_sb.shape[-1]
    x = x.astype(F32)
    for l in range(g_mix.shape[0]):
        u, q, k, v = _in_proj(x.reshape(n, d), g_mix[l], w_in[l], sb_g_q[l], sb_g_k[l], ssm_w, sb_w)
        mats = _s5_mats(ssm_a_re[l], ssm_a_im[l], ssm_log_dt[l], ssm_b_re[l], ssm_b_im[l],
                        ssm_c_re[l], ssm_c_im[l])
        y_s5 = _s5(u, mats, ssm_d[l], bsz)
        z_bound = (SB_HEAD_DIM * SB_Q_SCALE * SB_BOUND_SLACK
                   * jnp.max(jnp.abs(sb_g_q[l].astype(F32))) * jnp.max(jnp.abs(sb_g_k[l].astype(F32))))
        y_sb = _sb_attn(q.reshape(bsz, seq, sb_w), k.reshape(bsz, seq, sb_w), v.reshape(bsz, seq, sb_w), z_bound)
        mk, mv = _mem_kv(mem.astype(F32), g_mem[l], xa_w_kv[l], xa_g_k[l])
        x = _tail(x, y_s5.reshape(bsz, seq, ssm_w), y_sb, ssm_w_glu[l], g_out_ssm[l], g_out_sb[l], w_out[l],
                  g_xa[l], xa_w_q[l], xa_g_q[l], mk, mv, xa_w_o[l], g_mlp[l], w_up[l], w_down[l])
    return x
```

```python
import functools

import jax
import numpy as np
import jax.numpy as jnp
from jax import lax
from jax.experimental import pallas as pl
from jax.experimental.pallas import tpu as pltpu

F32 = jnp.float32
BF16 = jnp.bfloat16

NORM_EPS = 1e-6
SSM_GROUP = 16
SSM_STATE = 64
SSM_CHUNK = 16
S5_COL_GROUPS = 8
SB_HEAD_DIM = 64
XA_HEADS = 4
XA_HEAD_DIM = 128
LANES = 128
MXU_DIM = 256
LOG2E = 1.4426950408889634
SB_Q_SCALE = SB_HEAD_DIM ** -0.5 * LOG2E
SB_BOUND_SLACK = 1.05
VMEM_LIMIT_BYTES = 56 * 1024 * 1024


def _params(*sem):
    return pltpu.CompilerParams(dimension_semantics=sem, vmem_limit_bytes=VMEM_LIMIT_BYTES)


def _const_spec(shape):
    zeros = (0,) * len(shape)
    return pl.BlockSpec(shape, lambda *_: zeros, pipeline_mode=pl.Buffered(1))


def _rms(x, g):
    return x * lax.rsqrt(jnp.mean(x * x, axis=-1, keepdims=True) + NORM_EPS) * g


def _token_tile(n, want):
    t = min(n, want)
    assert n % t == 0
    return t


def _in_proj_kernel(x_ref, g_ref, w_ref, seg_ref, gq_ref, gk_ref,
                    u_ref, q_ref, k_ref, v_ref, *, ssm_w, sb_w):
    h = _rms(x_ref[...], g_ref[...])
    proj = jnp.dot(h.astype(BF16), w_ref[...], preferred_element_type=F32)
    u_ref[...] = proj[:, :ssm_w]
    q = proj[:, ssm_w:ssm_w + sb_w]
    k = proj[:, ssm_w + sb_w:ssm_w + 2 * sb_w]
    v_ref[...] = proj[:, ssm_w + 2 * sb_w:].astype(BF16)
    seg = seg_ref[...]

    def head_norm(t, g):
        sq = (t * t).astype(BF16)
        ms = jnp.concatenate([jnp.dot(sq[:, n * MXU_DIM:(n + 1) * MXU_DIM], seg, preferred_element_type=F32)
                              for n in range(t.shape[1] // MXU_DIM)], axis=1)
        return t * lax.rsqrt(ms + NORM_EPS) * g

    q_ref[...] = head_norm(q, gq_ref[...]).astype(BF16)
    k_ref[...] = head_norm(k, gk_ref[...]).astype(BF16)


def _in_proj(x2, g_mix, w_in, g_q, g_k, ssm_w, sb_w):
    n, d = x2.shape
    tm = _token_tile(n, 1024)
    heads = sb_w // SB_HEAD_DIM
    assert MXU_DIM % SB_HEAD_DIM == 0 and sb_w % MXU_DIM == 0
    lane_head = np.arange(MXU_DIM) // SB_HEAD_DIM
    seg = jnp.asarray((lane_head[:, None] == lane_head[None, :]) / SB_HEAD_DIM, BF16)
    gq = (jnp.tile(g_q.astype(F32), heads) * SB_Q_SCALE)[None]
    gk = jnp.tile(g_k.astype(F32), heads)[None]
    kern = functools.partial(_in_proj_kernel, ssm_w=ssm_w, sb_w=sb_w)
    tok = lambda w: pl.BlockSpec((tm, w), lambda i: (i, 0))
    return pl.pallas_call(
        kern,
        grid=(n // tm,),
        in_specs=[tok(d), _const_spec((1, d)), _const_spec(w_in.shape), _const_spec((MXU_DIM, MXU_DIM)),
                  _const_spec((1, sb_w)), _const_spec((1, sb_w))],
        out_specs=[tok(ssm_w), tok(sb_w), tok(sb_w), tok(sb_w)],
        out_shape=[jax.ShapeDtypeStruct((n, ssm_w), F32)] + [jax.ShapeDtypeStruct((n, sb_w), BF16)] * 3,
        compiler_params=_params("parallel"),
        name="in_proj",
    )(x2, g_mix.astype(F32)[None], w_in.astype(BF16), seg, gq, gk)


def _s5_mats(a_re, a_im, log_dt, b_re, b_im, c_re, c_im):
    hp = lax.Precision.HIGHEST
    L = SSM_CHUNK
    g, p = a_re.shape
    nc = b_re.shape[-1]
    gt = LANES // nc
    nt = g // gt
    lr, li = a_re.astype(F32), a_im.astype(F32)
    dt = jnp.exp(log_dt.astype(F32))[:, None]
    n = jnp.arange(L + 1, dtype=F32)[:, None, None]
    mag = jnp.exp(n * (lr * dt)[None])
    pr, pi = mag * jnp.cos(n * (li * dt)[None]), mag * jnp.sin(n * (li * dt)[None])
    xr, xi = pr[1] - 1.0, pi[1]
    den = lr * lr + li * li
    fr, fi = (xr * lr + xi * li) / den, (xi * lr - xr * li) / den
    brt = jnp.swapaxes(b_re.astype(F32), 1, 2)
    bit = jnp.swapaxes(b_im.astype(F32), 1, 2)
    bbr = fr[:, None, :] * brt - fi[:, None, :] * bit
    bbi = fr[:, None, :] * bit + fi[:, None, :] * brt
    cr, ci = c_re.astype(F32), c_im.astype(F32)
    mr = pr[:, :, None, :] * bbr[None] - pi[:, :, None, :] * bbi[None]
    mi = pr[:, :, None, :] * bbi[None] + pi[:, :, None, :] * bbr[None]
    taps = (jnp.einsum('ngdp,gcp->ngdc', mr[:L], cr, precision=hp)
            - jnp.einsum('ngdp,gcp->ngdc', mi[:L], ci, precision=hp))
    tile_rows = lambda a: a.reshape((L, nt, gt) + a.shape[2:])
    kt = tile_rows(taps).transpose(1, 0, 2, 3, 4).reshape(nt, L * LANES, nc)
    pcat = jnp.concatenate([mr[:L], mi[:L]], axis=-1)[::-1]
    pc = tile_rows(pcat).transpose(1, 0, 2, 3, 4).reshape(nt, L * LANES, 2 * p)
    qr = cr[None] * pr[1:, :, None, :] - ci[None] * pi[1:, :, None, :]
    qi = cr[None] * pi[1:, :, None, :] + ci[None] * pr[1:, :, None, :]
    qcat = jnp.concatenate([qr, -qi], axis=-1)
    qct = tile_rows(qcat).transpose(1, 0, 3, 2, 4).reshape(nt, L * nc, gt * 2 * p)
    coef = jnp.stack([jnp.concatenate([pr[L], pr[L]], -1),
                      jnp.concatenate([-pi[L], pi[L]], -1)], axis=1)
    cf = coef.reshape(nt, gt, 2, 2 * p).transpose(0, 2, 1, 3).reshape(nt, 2, 2 * p * gt)
    return kt.astype(BF16), pc.astype(BF16), qct.astype(BF16), cf.astype(F32)


def _spread_consts(nc):
    L = SSM_CHUNK
    lane = np.arange(LANES)
    to_tile = (np.arange(nc)[:, None] == (lane % nc)[None, :])
    col = np.arange(L * LANES)
    to_row = (np.arange(L * nc)[:, None] == (col // LANES * nc + col % nc)[None, :])
    return jnp.asarray(to_tile, BF16), jnp.asarray(to_row, BF16)


def _s5_kernel(u_ref, kt_ref, pc_ref, qct_ref, to_tile_ref, to_row_ref, coef_ref, d_ref, y_ref,
               tt_scr, pp_scr, qq_scr, e_scr, s_scr, *, sw, nc):
    L = SSM_CHUNK
    gt = LANES // nc
    lw = L * LANES

    @pl.when(pl.program_id(1) == 0)
    def _expand_tile_matrices():
        row_g = lax.broadcasted_iota(jnp.int32, (L * LANES, LANES), 0) // nc % gt
        col_g = lax.broadcasted_iota(jnp.int32, (L * LANES, LANES), 1) // nc
        taps = jnp.dot(kt_ref[0], to_tile_ref[...], preferred_element_type=F32)
        taps = jnp.where(row_g == col_g, taps, 0.0).astype(BF16)
        zero = jnp.zeros((LANES, LANES), BF16)
        for j in range(L):
            for i in range(L):
                lag = (i - j) * LANES
                tt_scr[j * LANES:(j + 1) * LANES, i * LANES:(i + 1) * LANES] = (
                    taps[lag:lag + LANES, :] if i >= j else zero)
        pc = pc_ref[0].astype(F32)
        for g in range(gt):
            pp_scr[:, g * LANES:(g + 1) * LANES] = jnp.where(row_g == g, pc, 0.0).astype(BF16)
        full = lax.dot_general(qct_ref[0], to_row_ref[...], (((0,), (0,)), ((), ())), preferred_element_type=F32)
        rg = lax.broadcasted_iota(jnp.int32, full.shape, 0) // LANES
        cg = lax.broadcasted_iota(jnp.int32, full.shape, 1) // nc % gt
        qq_scr[...] = jnp.where(rg == cg, full, 0.0).astype(BF16)

    nk = u_ref.shape[0]
    u_steps = jnp.swapaxes(u_ref[...], 0, 1)
    u = jnp.concatenate([u_steps[t] for t in range(L)], axis=1).astype(BF16)
    e_s = jnp.dot(u, pp_scr[...], preferred_element_type=F32)
    e_scr[:, :sw] = e_s
    for g in range(gt):
        e_scr[:, sw + g * LANES:sw + (g + 1) * LANES] = pltpu.roll(
            e_s[:, g * LANES:(g + 1) * LANES], LANES // 2, axis=1)
    c1 = coef_ref[0, 0:1, :]
    c2 = coef_ref[0, 1:2, :]

    def step(k, carry):
        s, w = carry
        s_scr[pl.ds(k, 1), :] = s
        e = e_scr[pl.ds(k, 1), :]
        return s * c1 + w * c2 + e[:, :sw], w * c1 - s * c2 + e[:, sw:]

    zero = jnp.zeros((1, sw), F32)
    lax.fori_loop(0, nk, step, (zero, zero), unroll=True)
    cw = lw // S5_COL_GROUPS
    y = jnp.concatenate(
        [jnp.dot(u[:, :(n + 1) * cw], tt_scr[:(n + 1) * cw, n * cw:(n + 1) * cw], preferred_element_type=F32)
         for n in range(S5_COL_GROUPS)], axis=1)
    y += jnp.dot(s_scr[...].astype(BF16), qq_scr[...], preferred_element_type=F32)
    y_steps = jnp.stack([y[:, i * LANES:(i + 1) * LANES] for i in range(L)], axis=0)
    y_ref[...] = jnp.swapaxes(y_steps, 0, 1) + d_ref[...] * u_ref[...]


def _s5(u, mats, d_skip, bsz):
    kt, pc, qct, cf = mats
    n, width = u.shape
    L = SSM_CHUNK
    nt = width // LANES
    nk = n // L // bsz
    lw, sw = L * LANES, cf.shape[-1]
    nc = kt.shape[-1]
    to_tile, to_row = _spread_consts(nc)
    kern = functools.partial(_s5_kernel, sw=sw, nc=nc)
    per_t = lambda a, b: pl.BlockSpec((1, a, b), lambda t, s: (t, 0, 0))
    seq_blk = pl.BlockSpec((nk, L, LANES), lambda t, s: (s, 0, t))
    y = pl.pallas_call(
        kern,
        grid=(nt, bsz),
        in_specs=[seq_blk, per_t(lw, nc), per_t(lw, LANES), per_t(L * nc, sw),
                  _const_spec(to_tile.shape), _const_spec(to_row.shape), per_t(2, sw), per_t(1, LANES)],
        out_specs=seq_blk,
        out_shape=jax.ShapeDtypeStruct((n // L, L, width), F32),
        scratch_shapes=[pltpu.VMEM((lw, lw), BF16), pltpu.VMEM((lw, sw), BF16), pltpu.VMEM((sw, lw), BF16),
                        pltpu.VMEM((nk, 2 * sw), F32), pltpu.VMEM((nk, sw), F32)],
        compiler_params=_params("parallel", "arbitrary"),
        name="s5_scan",
    )(u.reshape(n // L, L, width), kt, pc, qct, to_tile, to_row, cf, d_skip.astype(F32).reshape(nt, 1, LANES))
    return y.reshape(n, width)


F32_ZERO_EXP2 = 160.0
SB_WAVE = 1


def _sb_attn_kernel(thr_ref, q_ref, k_ref, v_ref, tri_ref, o_ref, *, blk, nlb):
    i = pl.program_id(1)
    lane = lax.broadcasted_iota(jnp.int32, (blk, LANES), 1)
    first = lane < SB_HEAD_DIM
    qs = []
    for c in range(nlb):
        q = q_ref[0, :, c * LANES:(c + 1) * LANES]
        zq = jnp.zeros_like(q)
        qs.append(jnp.concatenate([jnp.where(first, q, zq), jnp.where(first, zq, q)], axis=0))
    tri = tri_ref[...]
    row = lax.broadcasted_iota(jnp.int32, (2 * blk, blk), 0)
    col = lax.broadcasted_iota(jnp.int32, (2 * blk, blk), 1)
    below = col < jnp.where(row >= blk, row - blk, row)

    def block(j, carry, acc, mask):
        start = pl.multiple_of(j * blk, blk)
        new_acc, new_carry = list(acc), list(carry)
        for c0 in range(0, nlb, SB_WAVE):
            wave = range(c0, min(nlb, c0 + SB_WAVE))
            zs, ps = {}, {}
            for c in wave:
                kb = k_ref[0, pl.ds(start, blk), c * LANES:(c + 1) * LANES]
                zs[c] = lax.dot_general(qs[c], kb, (((1,), (1,)), ((), ())), preferred_element_type=F32)
            for c in wave:
                z = zs[c]
                sp = jnp.maximum(z, 0.0) + jnp.log(1.0 + jnp.exp2(-jnp.abs(z))) * LOG2E
                if mask is not None:
                    sp = jnp.where(mask, sp, 0.0)
                    zs[c] = jnp.where(mask, z, -1e30)
                ps[c] = sp.astype(BF16)
            cums = {c: jnp.dot(ps[c], tri, preferred_element_type=F32) for c in wave}
            ws = {c: jnp.exp2(zs[c] - cums[c] - carry[c]).astype(BF16) for c in wave}
            for c in wave:
                vb = v_ref[0, pl.ds(start, blk), c * LANES:(c + 1) * LANES]
                new_acc[c] = acc[c] + jnp.dot(ws[c], vb, preferred_element_type=F32)
                new_carry[c] = carry[c] + cums[c][:, 0:1]
        return tuple(new_carry), tuple(new_acc)

    zc = tuple(jnp.zeros((2 * blk, 1), F32) for _ in range(nlb))
    za = tuple(jnp.zeros((2 * blk, LANES), F32) for _ in range(nlb))
    carry, acc = block(i, zc, za, below)
    no_prev = jnp.where(i == 0, jnp.float32(1e30), jnp.float32(0.0))
    carry, acc = block(jnp.maximum(i - 1, 0), tuple(cv + no_prev for cv in carry), acc, None)
    thr = thr_ref[0]

    def row_min(carry):
        return jnp.min(functools.reduce(jnp.minimum, carry))

    def more(st):
        return jnp.logical_and(st[0] >= 0, st[1] <= thr)

    def step(st):
        j, _, carry, acc = st
        carry, acc = block(j, carry, acc, None)
        return j - 1, row_min(carry), carry, acc

    _, _, carry, acc = lax.while_loop(more, step, (i - 2, row_min(carry), carry, acc))
    for c in range(nlb):
        o_ref[0, :, c * LANES:(c + 1) * LANES] = jnp.where(first, acc[c][:blk], acc[c][blk:]).astype(o_ref.dtype)


def _sb_attn(q, k, v, z_bound):
    bsz, seq, width = q.shape
    blk = _token_tile(seq, 256)
    assert width % LANES == 0
    tri = jnp.asarray(np.arange(blk)[:, None] >= np.arange(blk)[None, :], BF16)
    thr = (z_bound + F32_ZERO_EXP2).astype(F32).reshape(1)
    kern = functools.partial(_sb_attn_kernel, blk=blk, nlb=width // LANES)
    q_blk = pl.BlockSpec((1, blk, width), lambda b, i: (b, i, 0))
    whole_seq = pl.BlockSpec((1, seq, width), lambda b, i: (b, 0, 0), pipeline_mode=pl.Buffered(1))
    return pl.pallas_call(
        kern,
        grid=(bsz, seq // blk),
        in_specs=[pl.BlockSpec(memory_space=pltpu.SMEM), q_blk, whole_seq, whole_seq,
                  _const_spec((blk, blk))],
        out_specs=q_blk,
        out_shape=jax.ShapeDtypeStruct((bsz, seq, width), BF16),
        compiler_params=_params("parallel", "arbitrary"),
        name="sb_attn",
    )(thr, q, k, v, tri)


def _mix_out_body(x, ys, sb, wg_ref, gs_ref, gb_ref, wo_ref):
    ssm_w = ys.shape[1]
    y = jax.nn.gelu(ys)
    gate = jnp.dot(y.astype(BF16), wg_ref[...], preferred_element_type=F32)
    y = y * jax.nn.sigmoid(gate)
    yn = _rms(y, gs_ref[...]).astype(BF16)
    sn = _rms(sb.astype(F32), gb_ref[...]).astype(BF16)
    out = jnp.dot(yn, wo_ref[:ssm_w, :], preferred_element_type=F32)
    out += jnp.dot(sn, wo_ref[ssm_w:, :], preferred_element_type=F32)
    return x + out


def _mem_kv_kernel(m_ref, g_ref, w_ref, gk_ref, k_ref, v_ref, *, xa_w):
    h = _rms(m_ref[0], g_ref[...])
    kv = jnp.dot(h.astype(BF16), w_ref[...], preferred_element_type=F32)
    mlen = kv.shape[0]
    for hd in range(xa_w // XA_HEAD_DIM):
        sl = slice(hd * XA_HEAD_DIM, (hd + 1) * XA_HEAD_DIM)
        k_ref[0, :, sl] = _rms(kv[:, sl], gk_ref[...]).astype(BF16)
    v_ref[...] = jnp.zeros_like(v_ref)
    for hd in range(xa_w // XA_HEAD_DIM):
        pair, side = divmod(hd, 2)
        v_ref[0, pair, side * mlen:(side + 1) * mlen, side * XA_HEAD_DIM:(side + 1) * XA_HEAD_DIM] = (
            kv[:, xa_w + hd * XA_HEAD_DIM:xa_w + (hd + 1) * XA_HEAD_DIM].astype(BF16))


def _mem_kv(mem, g_mem, w_kv, g_k):
    bsz, mlen, d = mem.shape
    xa_w = w_kv.shape[1] // 2
    pairs = xa_w // XA_HEAD_DIM // 2
    kern = functools.partial(_mem_kv_kernel, xa_w=xa_w)
    per_b = lambda w: pl.BlockSpec((1, mlen, w), lambda b: (b, 0, 0))
    return pl.pallas_call(
        kern,
        grid=(bsz,),
        in_specs=[per_b(d), _const_spec((1, d)), _const_spec(w_kv.shape), _const_spec((1, XA_HEAD_DIM))],
        out_specs=[per_b(xa_w), pl.BlockSpec((1, pairs, 2 * mlen, 2 * XA_HEAD_DIM), lambda b: (b, 0, 0, 0))],
        out_shape=[jax.ShapeDtypeStruct((bsz, mlen, xa_w), BF16),
                   jax.ShapeDtypeStruct((bsz, pairs, 2 * mlen, 2 * XA_HEAD_DIM), BF16)],
        compiler_params=_params("parallel"),
        name="mem_kv",
    )(mem, g_mem.astype(F32)[None], w_kv.astype(BF16), g_k.astype(F32)[None])


def _xattn_body(x, g_ref, wq_ref, gq_ref, k_ref, v_ref, wo_ref):
    h = _rms(x, g_ref[...])
    q = jnp.dot(h.astype(BF16), wq_ref[...], preferred_element_type=F32)
    es, sums = [], []
    for hd in range(q.shape[1] // XA_HEAD_DIM):
        sl = slice(hd * XA_HEAD_DIM, (hd + 1) * XA_HEAD_DIM)
        qh = _rms(q[:, sl], gq_ref[...]).astype(BF16)
        s = lax.dot_general(qh, k_ref[0, :, sl], (((1,), (1,)), ((), ())), preferred_element_type=F32)
        e = jnp.exp(s - jnp.max(s, axis=-1, keepdims=True))
        es.append(e.astype(BF16))
        sums.append(jnp.sum(e, axis=-1, keepdims=True))
    heads = []
    for pair in range(len(es) // 2):
        a, b = 2 * pair, 2 * pair + 1
        pv = jnp.dot(jnp.concatenate([es[a], es[b]], axis=1), v_ref[0, pair], preferred_element_type=F32)
        heads.append((pv[:, :XA_HEAD_DIM] / sums[a]).astype(BF16))
        heads.append((pv[:, XA_HEAD_DIM:] / sums[b]).astype(BF16))
    o = jnp.concatenate(heads, axis=-1)
    return x + jnp.dot(o, wo_ref[...], preferred_element_type=F32)


MLP_FF_CHUNK = 1024


def _mlp_body(x, g_ref, wu_ref, wd_ref):
    h = _rms(x, g_ref[...]).astype(BF16)
    ff = wu_ref.shape[1]
    ff_chunk = min(ff, MLP_FF_CHUNK)
    acc = x
    for c in range(ff // ff_chunk):
        sl = slice(c * ff_chunk, (c + 1) * ff_chunk)
        a = jnp.maximum(jnp.dot(h, wu_ref[:, sl], preferred_element_type=F32), 0.0)
        acc = acc + jnp.dot((a * a).astype(BF16), wd_ref[sl, :], preferred_element_type=F32)
    return acc


def _tail_kernel(x_ref, ys_ref, sb_ref, wg_ref, gs_ref, gb_ref, wo_ref,
                 gxa_ref, wq_ref, gq_ref, k_ref, v_ref, wxo_ref, gm_ref, wu_ref, wd_ref, o_ref):
    x = _mix_out_body(x_ref[0], ys_ref[0], sb_ref[0], wg_ref, gs_ref, gb_ref, wo_ref)
    x = _xattn_body(x, gxa_ref, wq_ref, gq_ref, k_ref, v_ref, wxo_ref)
    o_ref[0] = _mlp_body(x, gm_ref, wu_ref, wd_ref)


def _tail(x, y_s5, y_sb, w_glu, g_ssm, g_sb, w_out, g_xa, w_q, g_q, mk, mv, w_o, g_mlp, w_up, w_down):
    bsz, seq, d = x.shape
    ssm_w, sb_w = y_s5.shape[-1], y_sb.shape[-1]
    mlen, xa_w = mk.shape[1], mk.shape[2]
    tm = _token_tile(seq, 1024)
    gq = (g_q.astype(F32) * (XA_HEAD_DIM ** -0.5))[None]
    tok = lambda w: pl.BlockSpec((1, tm, w), lambda b, i: (b, i, 0))
    per_b = pl.BlockSpec((1, mlen, xa_w), lambda b, i: (b, 0, 0), pipeline_mode=pl.Buffered(1))
    v_pairs = pl.BlockSpec((1,) + mv.shape[1:], lambda b, i: (b, 0, 0, 0), pipeline_mode=pl.Buffered(1))
    vec = lambda g: g.astype(F32)[None]
    return pl.pallas_call(
        _tail_kernel,
        grid=(bsz, seq // tm),
        in_specs=[tok(d), tok(ssm_w), tok(sb_w),
                  _const_spec(w_glu.shape), _const_spec((1, ssm_w)), _const_spec((1, sb_w)), _const_spec(w_out.shape),
                  _const_spec((1, d)), _const_spec(w_q.shape), _const_spec((1, XA_HEAD_DIM)), per_b, v_pairs,
                  _const_spec(w_o.shape),
                  _const_spec((1, d)), _const_spec(w_up.shape), _const_spec(w_down.shape)],
        out_specs=tok(d),
        out_shape=jax.ShapeDtypeStruct((bsz, seq, d), F32),
        compiler_params=_params("parallel", "parallel"),
        name="tail",
    )(x, y_s5, y_sb, w_glu.astype(BF16), vec(g_ssm), vec(g_sb), w_out.astype(BF16),
      vec(g_xa), w_q.astype(BF16), gq, mk, mv, w_o.astype(BF16),
      vec(g_mlp), w_up.astype(BF16), w_down.astype(BF16))


def kernel(x, mem, g_mix, w_in, ssm_a_re, ssm_a_im, ssm_log_dt, ssm_b_re, ssm_b_im, ssm_c_re, ssm_c_im,
           ssm_d, ssm_w_glu, sb_g_q, sb_g_k, g_out_ssm, g_out_sb, w_out, g_xa, g_mem, xa_w_q, xa_w_kv,
           xa_g_q, xa_g_k, xa_w_o, g_mlp, w_up, w_down):
    bsz, seq, d = x.shape
    n = bsz * seq
    ssm_w = ssm_d.shape[-1]
    sb_w = g_out_sb.shape[-1]
    x = x.astype(F32)
    for l in range(g_mix.shape[0]):
        u, q, k, v = _in_proj(x.reshape(n, d), g_mix[l], w_in[l], sb_g_q[l], sb_g_k[l], ssm_w, sb_w)
        mats = _s5_mats(ssm_a_re[l], ssm_a_im[l], ssm_log_dt[l], ssm_b_re[l], ssm_b_im[l],
                        ssm_c_re[l], ssm_c_im[l])
        y_s5 = _s5(u, mats, ssm_d[l], bsz)
        z_bound = (SB_HEAD_DIM * SB_Q_SCALE * SB_BOUND_SLACK
                   * jnp.max(jnp.abs(sb_g_q[l].astype(F32))) * jnp.max(jnp.abs(sb_g_k[l].astype(F32))))
        y_sb = _sb_attn(q.reshape(bsz, seq, sb_w), k.reshape(bsz, seq, sb_w), v.reshape(bsz, seq, sb_w), z_bound)
        mk, mv = _mem_kv(mem.astype(F32), g_mem[l], xa_w_kv[l], xa_g_k[l])
        x = _tail(x, y_s5.reshape(bsz, seq, ssm_w), y_sb, ssm_w_glu[l], g_out_ssm[l], g_out_sb[l], w_out[l],
                  g_xa[l], xa_w_q[l], xa_g_q[l], mk, mv, xa_w_o[l], g_mlp[l], w_up[l], w_down[l])
    return x
```
